```python
import math
import jax, jax.numpy as jnp
from jax import lax
import numpy as np

D_MODEL = 1024
BATCH = 2
SEQ = 8192
DEPTH = 2
DEC_BATCH = 128
DEC_SEQ = 4
PAST_LEN = 2048
PAGE_SIZE = 128

HEAD_DIM = 64
H_A = 8
BRANCHES = ((128, 1), (512, 4), (2048, 16))
MAX_WINDOW = 2048
ATT_BLOCK = 128
H_B = 4
DK_B = 32
DV_B = 64
GATE_RANK = 16
GATE_TAU = 16.0
H_C = 4
DK_C = 32
DV_C = 64
LA_CHUNK = 16
W_A = H_A * HEAD_DIM
QK_B = H_B * DK_B
V_B = H_B * DV_B
QK_C = H_C * DK_C
V_C = H_C * DV_C
MIX_WIDTH = W_A + V_B + V_C
D_IN = 3 * W_A + 2 * QK_B + 2 * V_B + GATE_RANK + 2 * QK_C + 2 * V_C
N_KEYS = 128
N_EXPERTS = N_KEYS * N_KEYS
PEER_HEADS = 8
PEER_DKEY = 256
PEER_TOPK = 16
PEER_BLOCK = 128
NORM_EPS = 1e-6

kernel_name = "hymba_dilated_gla_retnet_peer_step"


def rms_norm(x, g):
    xf = x.astype(jnp.float32)
    y = xf * lax.rsqrt(jnp.mean(xf * xf, axis=-1, keepdims=True) + NORM_EPS)
    return (y * g.astype(jnp.float32)).astype(x.dtype)


def alibi_slopes():
    return 2.0 ** (-8.0 * jnp.arange(1, H_A + 1, dtype=jnp.float32) / H_A)


def split_columns(z):
    sizes = (W_A, W_A, W_A, QK_B, QK_B, V_B, V_B, GATE_RANK, QK_C, QK_C, V_C, V_C)
    out, start = [], 0
    for n in sizes:
        out.append(z[..., start:start + n])
        start += n
    return out


def dilated_branch_prompt(q, k, v, window, dilation, slopes):
    B, S, H, Dh = q.shape
    L = S // dilation
    steps = window // dilation
    nb = -(-L // ATT_BLOCK)
    Lp = nb * ATT_BLOCK

    def to_blocks(t):
        t = t.reshape(B, L, dilation, H, Dh).transpose(0, 2, 1, 3, 4)
        t = jnp.pad(t, ((0, 0), (0, 0), (0, Lp - L), (0, 0), (0, 0)))
        return t.reshape(B, dilation, nb, ATT_BLOCK, H, Dh)

    def with_prev(t):
        prev = jnp.pad(t, ((0, 0), (0, 0), (1, 0), (0, 0), (0, 0), (0, 0)))[:, :, :-1]
        return jnp.concatenate([prev, t], axis=3)

    qb = to_blocks(q)
    kk = with_prev(to_blocks(k))
    vv = with_prev(to_blocks(v))
    s = jnp.einsum('brnqhd,brnkhd->brnhqk', qb, kk, preferred_element_type=jnp.float32) * (Dh ** -0.5)
    qi = jnp.arange(ATT_BLOCK)[:, None]
    ki = jnp.arange(2 * ATT_BLOCK)[None, :]
    dist = qi + ATT_BLOCK - ki
    key_elem = jnp.arange(nb)[:, None, None] * ATT_BLOCK - ATT_BLOCK + ki[None]
    valid = (dist >= 0) & (dist <= steps) & (key_elem >= 0)
    bias = -slopes[:, None, None] * (dist * dilation).astype(jnp.float32)[None]
    s = jnp.where(valid[:, None], s + bias, -jnp.inf)
    m = jnp.max(s, axis=-1, keepdims=True)
    p = jnp.exp(s - m)
    den = jnp.sum(p, axis=-1, keepdims=True)
    o = jnp.einsum('brnhqk,brnkhd->brnqhd', p / den, vv)
    lse = (m + jnp.log(den))[..., 0]
    o = o.reshape(B, dilation, Lp, H, Dh)[:, :, :L].transpose(0, 2, 1, 3, 4).reshape(B, S, H, Dh)
    lse = lse.transpose(0, 1, 2, 4, 3).reshape(B, dilation, Lp, H)[:, :, :L]
    lse = lse.transpose(0, 2, 1, 3).reshape(B, S, H)
    return o, lse


def dilated_branch_sample(q, k_all, v_all, window, dilation, slopes):
    Bd, T, H, Dh = q.shape
    n_past = k_all.shape[1] - T
    j = jnp.arange(window // dilation + 1)
    key_idx = n_past + jnp.arange(T)[:, None] - j[None, :] * dilation
    valid = key_idx >= 0
    idx = jnp.maximum(key_idx, 0)
    kg = jnp.take(k_all, idx, axis=1)
    vg = jnp.take(v_all, idx, axis=1)
    s = jnp.einsum('bthd,btjhd->bthj', q, kg, preferred_element_type=jnp.float32) * (Dh ** -0.5)
    s = s - slopes[:, None] * (j * dilation).astype(jnp.float32)[None, :]
    s = jnp.where(valid[:, None, :], s, -jnp.inf)
    m = jnp.max(s, axis=-1, keepdims=True)
    p = jnp.exp(s - m)
    den = jnp.sum(p, axis=-1, keepdims=True)
    o = jnp.einsum('bthj,btjhd->bthd', p / den, vg)
    return o, (m + jnp.log(den))[..., 0]


def combine_branches(outs, lses):
    w = jax.nn.softmax(jnp.stack(lses, axis=0), axis=0)
    return jnp.einsum('gbth,gbthd->bthd', w, jnp.stack(outs, axis=0))


def chunked_gated_linear_attn(q, k, v, log_a, s0):
    B, T, H, Dk = q.shape
    Dv = v.shape[-1]
    C = LA_CHUNK
    N = -(-T // C)
    pad = N * C - T

    def prep(t):
        t = jnp.pad(t.astype(jnp.float32), ((0, 0), (0, pad), (0, 0), (0, 0)))
        return t.reshape(B, N, C, H, t.shape[-1])

    qc, kc, vc, la = prep(q), prep(k), prep(v), prep(log_a)
    b = jnp.cumsum(la, axis=2)
    causal = jnp.tril(jnp.ones((C, C), dtype=bool))
    diff = b[:, :, :, None] - b[:, :, None, :]
    decay = jnp.exp(jnp.where(causal[None, None, :, :, None, None], diff, -jnp.inf))
    att = jnp.einsum('bnihd,bnjhd,bnijhd->bnhij', qc, kc, decay)
    o_intra = jnp.einsum('bnhij,bnjhe->bnihe', att, vc)
    b_last = b[:, :, -1]
    kv = jnp.einsum('bnjhd,bnjhe->bnhde', kc * jnp.exp(b_last[:, :, None] - b), vc)

    def step(S, inp):
        kv_n, bl = inp
        return jnp.exp(bl)[..., None] * S + kv_n, S

    s_fin, s_before = lax.scan(step, s0.astype(jnp.float32),
                               (kv.transpose(1, 0, 2, 3, 4), b_last.transpose(1, 0, 2, 3)))
    s_before = s_before.transpose(1, 0, 2, 3, 4)
    o_inter = jnp.einsum('bnihd,bnhde->bnihe', qc * jnp.exp(b), s_before)
    o = (o_intra + o_inter).reshape(B, N * C, H, Dv)[:, :T]
    return o, s_fin


def token_mixer(h, w_in, w_gate2, b_gate, g_gla, w_out, kv_past, s_gla0, s_ret0):
    Bn, T, _ = h.shape
    qa, ka, va, qb, kb, vb, rb, gb, qc, kc, vc, gc = split_columns(h @ w_in)
    heads = lambda t, n: t.reshape(Bn, T, n, t.shape[-1] // n)
    qa, ka, va = heads(qa, H_A), heads(ka, H_A), heads(va, H_A)
    slopes = alibi_slopes()
    if kv_past is None:
        res = [dilated_branch_prompt(qa, ka, va, w, d, slopes) for (w, d) in BRANCHES]
        kv_state = jnp.stack([ka, va], axis=2)[:, T - min(MAX_WINDOW, T):]
    else:
        k_all = jnp.concatenate([kv_past[:, :, 0].astype(ka.dtype), ka], axis=1)
        v_all = jnp.concatenate([kv_past[:, :, 1].astype(va.dtype), va], axis=1)
        res = [dilated_branch_sample(qa, k_all, v_all, w, d, slopes) for (w, d) in BRANCHES]
        kv_state = jnp.stack([ka, va], axis=2)
    o_a = combine_branches([r[0] for r in res], [r[1] for r in res])
    la_b = jax.nn.log_sigmoid((gb @ w_gate2 + b_gate).astype(jnp.float32)) / GATE_TAU
    o_b, s_gla = chunked_gated_linear_attn(heads(qb, H_B) * (DK_B ** -0.5), heads(kb, H_B),
                                           heads(vb, H_B), heads(la_b, H_B), s_gla0)
    o_b = o_b * lax.rsqrt(jnp.mean(o_b * o_b, axis=-1, keepdims=True) + NORM_EPS) * g_gla.astype(jnp.float32)
    o_b = o_b.reshape(Bn, T, V_B) * jax.nn.silu(rb.astype(jnp.float32))
    log_gamma = jnp.log(1.0 - 2.0 ** (-5.0 - jnp.arange(H_C, dtype=jnp.float32)))
    la_c = jnp.broadcast_to(log_gamma[None, None, :, None], (Bn, T, H_C, DK_C))
    o_c, s_ret = chunked_gated_linear_attn(heads(qc, H_C) * (DK_C ** -0.5), heads(kc, H_C),
                                           heads(vc, H_C), la_c, s_ret0)
    mu = jnp.mean(o_c, axis=-1, keepdims=True)
    o_c = (o_c - mu) * lax.rsqrt(jnp.mean(jnp.square(o_c - mu), axis=-1, keepdims=True) + NORM_EPS)
    o_c = o_c.reshape(Bn, T, V_C) * jax.nn.silu(gc.astype(jnp.float32))
    o = jnp.concatenate([o_a.reshape(Bn, T, W_A), o_b, o_c], axis=-1).astype(h.dtype)
    return (o @ w_out).astype(h.dtype), kv_state, s_gla, s_ret


def peer_ffn(h, w_pq, sub_keys, u_tab, v_tab):
    Bn, T, D = h.shape
    xt = h.reshape(-1, D)
    n = xt.shape[0]
    npad = -(-n // PEER_BLOCK) * PEER_BLOCK
    xt = jnp.pad(xt, ((0, npad - n), (0, 0)))
    keys = sub_keys.astype(jnp.float32)

    def block(xb):
        qk = (xb @ w_pq).astype(jnp.float32).reshape(PEER_BLOCK, PEER_HEADS, 2, PEER_DKEY // 2)
        sc = jnp.einsum('thpc,pkc->thpk', qk, keys)
        s1, i1 = lax.top_k(sc[:, :, 0], PEER_TOPK)
        s2, i2 = lax.top_k(sc[:, :, 1], PEER_TOPK)
        cand = (s1[..., :, None] + s2[..., None, :]).reshape(PEER_BLOCK, PEER_HEADS, PEER_TOPK * PEER_TOPK)
        cidx = (i1[..., :, None] * N_KEYS + i2[..., None, :]).reshape(PEER_BLOCK, PEER_HEADS, PEER_TOPK * PEER_TOPK)
        top_s, pos = lax.top_k(cand, PEER_TOPK)
        eidx = jnp.take_along_axis(cidx, pos, axis=-1).reshape(PEER_BLOCK, PEER_HEADS * PEER_TOPK)
        gate = jax.nn.softmax(top_s, axis=-1).reshape(PEER_BLOCK, PEER_HEADS * PEER_TOPK)
        u = jnp.take(u_tab, eidx, axis=0)
        v = jnp.take(v_tab, eidx, axis=0)
        pre = jnp.einsum('td,ted->te', xb, u, preferred_element_type=jnp.float32)
        act = jax.nn.gelu(pre, approximate=False) * gate
        return jnp.einsum('te,ted->td', act, v).astype(h.dtype)

    out = lax.map(block, xt.reshape(-1, PEER_BLOCK, D)).reshape(npad, D)[:n]
    return out.reshape(Bn, T, D)


def trunk(x, kv_win, s_gla, s_ret, w_in, w_gate2, b_gate, g_gla, w_out, g_mix, g_ffn,
          w_pq, sub_keys, u_tab, v_tab, g_final):
    Bn = x.shape[0]
    kv_rows, gla_states, ret_states = [], [], []
    for l in range(DEPTH):
        if kv_win is None:
            kv_l = None
            sg0 = jnp.zeros((Bn, H_B, DK_B, DV_B), jnp.float32)
            sr0 = jnp.zeros((Bn, H_C, DK_C, DV_C), jnp.float32)
        else:
            kv_l, sg0, sr0 = kv_win[l], s_gla[l], s_ret[l]
        y, kv_new, sg, sr = token_mixer(rms_norm(x, g_mix[l]), w_in[l], w_gate2[l], b_gate[l],
                                        g_gla[l], w_out[l], kv_l, sg0, sr0)
        x = x + y
        x = x + peer_ffn(rms_norm(x, g_ffn[l]), w_pq[l], sub_keys[l], u_tab[l], v_tab[l])
        kv_rows.append(kv_new)
        gla_states.append(sg)
        ret_states.append(sr)
    return rms_norm(x, g_final), jnp.stack(kv_rows), jnp.stack(gla_states), jnp.stack(ret_states)


def setup_inputs(seed: int = 0) -> dict:
    key = jax.random.key(seed)
    ks = jax.random.split(key, 18)
    f32 = jnp.float32
    w_buf = min(MAX_WINDOW, PAST_LEN)
    nrm = lambda k, shape, s: jax.random.normal(k, shape, f32) * s
    return {
        "x_prompt": nrm(ks[0], (BATCH, SEQ, D_MODEL), 1.0),
        "x_sample": nrm(ks[1], (DEC_BATCH, DEC_SEQ, D_MODEL), 1.0),
        "cache_kv_win": nrm(ks[2], (DEPTH, DEC_BATCH, w_buf, 2, H_A, HEAD_DIM), 1.0),
        "state_gla": nrm(ks[3], (DEPTH, DEC_BATCH, H_B, DK_B, DV_B), 1.0),
        "state_ret": nrm(ks[4], (DEPTH, DEC_BATCH, H_C, DK_C, DV_C), 1.0),
        "w_in": nrm(ks[5], (DEPTH, D_MODEL, D_IN), D_MODEL ** -0.5),
        "w_gate2": nrm(ks[6], (DEPTH, GATE_RANK, QK_B), GATE_RANK ** -0.5),
        "b_gate": nrm(ks[7], (DEPTH, QK_B), 0.1),
        "g_gla": 1.0 + nrm(ks[8], (DEPTH, DV_B), 0.01),
        "w_out": nrm(ks[9], (DEPTH, MIX_WIDTH, D_MODEL), MIX_WIDTH ** -0.5),
        "g_mix": 1.0 + nrm(ks[10], (DEPTH, D_MODEL), 0.01),
        "g_ffn": 1.0 + nrm(ks[11], (DEPTH, D_MODEL), 0.01),
        "w_pq": nrm(ks[12], (DEPTH, D_MODEL, PEER_HEADS * PEER_DKEY), D_MODEL ** -0.5),
        "sub_keys": nrm(ks[13], (DEPTH, 2, N_KEYS, PEER_DKEY // 2), (PEER_DKEY // 2) ** -0.5),
        "u_tab": nrm(ks[14], (DEPTH, N_EXPERTS, D_MODEL), D_MODEL ** -0.5),
        "v_tab": nrm(ks[15], (DEPTH, N_EXPERTS, D_MODEL), (PEER_HEADS * PEER_TOPK) ** -0.5),
        "g_final": 1.0 + nrm(ks[16], (D_MODEL,), 0.01),
    }


def reference(x_prompt, x_sample, cache_kv_win, state_gla, state_ret, w_in, w_gate2, b_gate, g_gla,
              w_out, g_mix, g_ffn, w_pq, sub_keys, u_tab, v_tab, g_final):
    y_prompt, kv_win_prompt, gla_prompt, ret_prompt = trunk(
        x_prompt, None, None, None, w_in, w_gate2, b_gate, g_gla, w_out, g_mix, g_ffn,
        w_pq, sub_keys, u_tab, v_tab, g_final)
    y_sample, kv_win_sample, gla_sample, ret_sample = trunk(
        x_sample, cache_kv_win, state_gla, state_ret, w_in, w_gate2, b_gate, g_gla, w_out, g_mix, g_ffn,
        w_pq, sub_keys, u_tab, v_tab, g_final)
    return (y_prompt, y_sample, kv_win_prompt, kv_win_sample, gla_prompt, gla_sample, ret_prompt, ret_sample)
```

```python
import functools
import math

import numpy as np
import jax
import jax.numpy as jnp
from jax import lax
from jax.experimental import pallas as pl
from jax.experimental.pallas import tpu as pltpu

F32 = jnp.float32
BF16 = jnp.bfloat16
HIGHEST = lax.Precision.HIGHEST

D_MODEL = 1024
HEAD_DIM = 64
H_A = 8
W_A = H_A * HEAD_DIM
BRANCHES = ((128, 1), (512, 4), (2048, 16))
BAND = 128
MAX_WINDOW = 2048
H_L = 4
DK_L = 32
DV_L = 64
QK_L = H_L * DK_L
V_L = H_L * DV_L
GATE_RANK = 16
GATE_TAU = 16.0
N_KEYS = 128
N_EXPERTS = N_KEYS * N_KEYS
PEER_HEADS = 8
PEER_TOPK = 16
NORM_EPS = 1e-6
W_LIN = 2 * 2 * QK_L + 128 + 2 * V_L + 2 * V_L

LANE = 128
SUBLANE = 8
VMEM_LIMIT = 56 * 1024 * 1024


def _params(*sem):
    return pltpu.CompilerParams(dimension_semantics=sem, vmem_limit_bytes=VMEM_LIMIT)


def _rms(x, g):
    return x * lax.rsqrt(jnp.mean(x * x, axis=-1, keepdims=True) + NORM_EPS) * g


def _nt(a, b, **kw):
    return lax.dot_general(a, b, (((1,), (1,)), ((), ())), preferred_element_type=F32, **kw)


def _tn(a, b, **kw):
    return lax.dot_general(a, b, (((0,), (0,)), ((), ())), preferred_element_type=F32, **kw)


def _nn(a, b, **kw):
    return jnp.dot(a, b, preferred_element_type=F32, **kw)


def _in_proj_kernel(x_ref, g_ref, wa_ref, wl_ref, za_ref, zl_ref):
    h = _rms(x_ref[...], g_ref[...]).astype(BF16)
    za_ref[...] = _nn(h, wa_ref[...])
    zl_ref[...] = _nn(h, wl_ref[...])


def _in_proj(x, g, wa, wl, tm=256):
    t = x.shape[0]
    return pl.pallas_call(
        _in_proj_kernel,
        grid=(t // tm,),
        in_specs=[
            pl.BlockSpec((tm, D_MODEL), lambda i: (i, 0)),
            pl.BlockSpec((1, D_MODEL), lambda i: (0, 0)),
            pl.BlockSpec((D_MODEL, 3 * W_A), lambda i: (0, 0)),
            pl.BlockSpec((D_MODEL, W_LIN), lambda i: (0, 0)),
        ],
        out_specs=[
            pl.BlockSpec((tm, 3 * W_A), lambda i: (i, 0)),
            pl.BlockSpec((tm, W_LIN), lambda i: (i, 0)),
        ],
        out_shape=[jax.ShapeDtypeStruct((t, 3 * W_A), F32), jax.ShapeDtypeStruct((t, W_LIN), F32)],
        compiler_params=_params("parallel"),
        name="in_proj",
    )(x, g, wa, wl)


def _in_proj_sample_kernel(x_ref, g_ref, wa_ref, wlt_ref, za_ref, zlt_ref):
    h = _rms(x_ref[...], g_ref[...]).astype(BF16)
    za_ref[...] = _nn(h, wa_ref[...])
    zlt_ref[...] = _nt(wlt_ref[...], h)


def _in_proj_sample(x, g, wa, wlt):
    t = x.shape[0]
    return pl.pallas_call(
        _in_proj_sample_kernel,
        out_shape=[jax.ShapeDtypeStruct((t, 3 * W_A), F32), jax.ShapeDtypeStruct((W_LIN, t), F32)],
        compiler_params=pltpu.CompilerParams(vmem_limit_bytes=VMEM_LIMIT),
        name="in_proj_sample",
    )(x, g, wa, wlt)


def _alibi_slope(h):
    return 2.0 ** (-8.0 * (h + 1) / H_A)


def _attn_prompt_kernel(q_ref, kc_ref, kp_ref, vc_ref, vp_ref, o_ref, lse_ref, *, dilation):
    i = pl.program_id(2)
    qi = lax.broadcasted_iota(jnp.int32, (BAND, 2 * BAND), 0)
    ki = lax.broadcasted_iota(jnp.int32, (BAND, 2 * BAND), 1)
    dist = qi + BAND - ki
    first_key = jnp.where(i > 0, 0, BAND)
    valid = (dist >= 0) & (dist <= BAND) & (ki >= first_key)
    distf = dist.astype(F32) * float(dilation)
    low = lax.broadcasted_iota(jnp.int32, (BAND, LANE), 1) < HEAD_DIM
    for p in range(H_A // 2):
        sl = slice(p * LANE, (p + 1) * LANE)
        q = q_ref[:, sl]
        k = jnp.concatenate([kp_ref[:, sl], kc_ref[:, sl]], axis=0).astype(BF16)
        v = jnp.concatenate([vp_ref[:, sl], vc_ref[:, sl]], axis=0).astype(BF16)
        outs, lses = [], []
        for sub in range(2):
            slope = _alibi_slope(2 * p + sub)
            qm = jnp.where(low if sub == 0 else jnp.logical_not(low), q, 0.0).astype(BF16)
            s = _nt(qm, k) * (HEAD_DIM ** -0.5)
            s = jnp.where(valid, s - slope * distf, -jnp.inf)
            m = jnp.max(s, axis=-1, keepdims=True)
            e = jnp.exp(s - m)
            den = jnp.sum(e, axis=-1, keepdims=True)
            outs.append(_nn(e.astype(BF16), v) / den)
            lses.append(jnp.broadcast_to(m + jnp.log(den), (BAND, LANE)))
        o_ref[:, sl] = jnp.where(low, outs[0], outs[1])
        lse_ref[:, sl] = jnp.where(low, lses[0], lses[1])


def _attn_prompt_branch(za, batch, seq, dilation):
    sub_len = seq // dilation
    nblk = sub_len // BAND
    zav = za.reshape(batch, sub_len, dilation * 3 * W_A)
    blk = (None, BAND, W_A)
    prev = lambda i: jnp.maximum(i - 1, 0)
    o, lse = pl.pallas_call(
        functools.partial(_attn_prompt_kernel, dilation=dilation),
        grid=(batch, dilation, nblk),
        in_specs=[
            pl.BlockSpec(blk, lambda b, r, i: (b, i, 3 * r)),
            pl.BlockSpec(blk, lambda b, r, i: (b, i, 3 * r + 1)),
            pl.BlockSpec(blk, lambda b, r, i: (b, prev(i), 3 * r + 1)),
            pl.BlockSpec(blk, lambda b, r, i: (b, i, 3 * r + 2)),
            pl.BlockSpec(blk, lambda b, r, i: (b, prev(i), 3 * r + 2)),
        ],
        out_specs=[
            pl.BlockSpec(blk, lambda b, r, i: (b, i, r)),
            pl.BlockSpec(blk, lambda b, r, i: (b, i, r)),
        ],
        out_shape=[jax.ShapeDtypeStruct((batch, sub_len, dilation * W_A), F32)] * 2,
        compiler_params=_params("parallel", "parallel", "arbitrary"),
        name=f"attn_prompt_d{dilation}",
    )(zav, zav, zav, zav, zav)
    return o.reshape(batch * seq, W_A), lse.reshape(batch * seq, W_A)


STRIDED_GROUPS = 96
TAIL_ROWS = 512


def _sample_attn_tables(t_new):
    n_past = MAX_WINDOW

    def entry(h, t, idx):
        if t >= t_new:
            return 0.0
        delta = n_past + t - idx
        if delta < 0:
            return -np.inf
        cnt = sum(1 for (w, d) in BRANCHES if delta % d == 0 and delta <= w)
        if cnt == 0:
            return -np.inf
        return -_alibi_slope(h) * delta + math.log(cnt)

    def table(idxs):
        tab = np.zeros((len(idxs), H_A * SUBLANE), np.float32)
        for r, idx in enumerate(idxs):
            for h in range(H_A):
                for t in range(SUBLANE):
                    if idx is None:
                        tab[r, h * SUBLANE + t] = 0.0 if t >= t_new else -np.inf
                    else:
                        tab[r, h * SUBLANE + t] = entry(h, t, idx)
        return tab

    strided = np.stack([table([16 * i + u for i in range(STRIDED_GROUPS)]) for u in range(4)])
    tail = table(list(range(MAX_WINDOW - TAIL_ROWS, MAX_WINDOW)))
    new = table([n_past + t if t < t_new else None for t in range(SUBLANE)])
    return jnp.asarray(strided), jnp.asarray(tail), jnp.asarray(new)


def _attn_sample_kernel(za_ref, kvs_ref, kvt_ref, bs_ref, bt_ref, bn_ref, o_ref, *, t_new):
    zero_rows = jnp.zeros((SUBLANE - t_new, W_A), F32)
    rows = [za_ref[t] for t in range(t_new)]
    q8 = jnp.concatenate([r[:, 0:W_A] for r in rows] + [zero_rows], axis=0)
    k8 = jnp.concatenate([r[:, W_A:2 * W_A] for r in rows] + [zero_rows], axis=0)
    v8 = jnp.concatenate([r[:, 2 * W_A:3 * W_A] for r in rows] + [zero_rows], axis=0)
    head_of_lane = lax.broadcasted_iota(jnp.int32, (SUBLANE, W_A), 1) // HEAD_DIM
    qbd = jnp.concatenate([jnp.where(head_of_lane == h, q8, 0.0) for h in range(H_A)], axis=0).astype(BF16)

    pieces = []
    for u in range(4):
        base = u * 2 * W_A
        pieces.append((kvs_ref[:, base:base + W_A], kvs_ref[:, base + W_A:base + 2 * W_A], bs_ref[u]))
    pieces.append((kvt_ref[:, 0:W_A], kvt_ref[:, W_A:2 * W_A], bt_ref[...]))
    pieces.append((k8, v8, bn_ref[...]))

    scores = [_nt(k.astype(BF16), qbd) * (HEAD_DIM ** -0.5) + bias for (k, _, bias) in pieces]
    m = scores[0].max(axis=0, keepdims=True)
    for s in scores[1:]:
        m = jnp.maximum(m, s.max(axis=0, keepdims=True))
    acc = jnp.zeros((H_A * SUBLANE, W_A), F32)
    den = jnp.zeros((H_A * SUBLANE, LANE), F32)
    for s, (_, v, _) in zip(scores, pieces):
        p = jnp.exp(s - m).astype(BF16)
        acc = acc + _tn(p, v.astype(BF16))
        den = den + _tn(p, jnp.ones((p.shape[0], LANE), BF16))
    acc = acc / den[:, 0:1]
    out = jnp.zeros((SUBLANE, W_A), F32)
    for h in range(H_A):
        out = out + jnp.where(head_of_lane == h, acc[h * SUBLANE:(h + 1) * SUBLANE, :], 0.0)
    for t in range(t_new):
        o_ref[t] = out[t:t + 1, :]


def _attn_sample(za, cache_l, tables, n_seq, t_new):
    bs, bt, bn = tables
    zav = za.reshape(t_new, n_seq, 1, 3 * W_A)
    kv_grouped = cache_l.reshape(n_seq, MAX_WINDOW // 16, 16 * 2 * W_A)
    kv_rows = cache_l.reshape(n_seq, MAX_WINDOW, 2 * W_A)
    o = pl.pallas_call(
        functools.partial(_attn_sample_kernel, t_new=t_new),
        grid=(n_seq,),
        in_specs=[
            pl.BlockSpec((t_new, None, 1, 3 * W_A), lambda b: (0, b, 0, 0)),
            pl.BlockSpec((None, STRIDED_GROUPS, 4 * 2 * W_A), lambda b: (b, 0, 0)),
            pl.BlockSpec((None, TAIL_ROWS, 2 * W_A), lambda b: (b, MAX_WINDOW // TAIL_ROWS - 1, 0)),
            pl.BlockSpec(bs.shape, lambda b: (0, 0, 0)),
            pl.BlockSpec(bt.shape, lambda b: (0, 0)),
            pl.BlockSpec(bn.shape, lambda b: (0, 0)),
        ],
        out_specs=pl.BlockSpec((t_new, None, 1, W_A), lambda b: (0, b, 0, 0)),
        out_shape=jax.ShapeDtypeStruct((t_new, n_seq, 1, W_A), F32),
        compiler_params=_params("parallel"),
        name="attn_sample",
    )(zav, kv_grouped, kv_rows, bs, bt, bn)
    return o.reshape(t_new * n_seq, W_A)


CHUNK = 128


def _linattn_prompt_kernel(zl_ref, wg_ref, bg_ref, lgam_ref, ggla_ref, obc_ref, sfin_ref, s_scr):
    j = pl.program_id(1)

    @pl.when(j == 0)
    def _():
        s_scr[...] = jnp.zeros_like(s_scr)

    c = CHUNK
    z = zl_ref[...]
    q = z[:, 0:2 * QK_L] * (DK_L ** -0.5)
    k = z[:, 2 * QK_L:4 * QK_L]
    gb = z[:, 4 * QK_L:4 * QK_L + LANE]
    v = z[:, 4 * QK_L + LANE:4 * QK_L + LANE + 2 * V_L]
    gates = z[:, 4 * QK_L + LANE + 2 * V_L:]

    pre = _nn(gb, wg_ref[...], precision=HIGHEST) + bg_ref[...]
    la = jnp.concatenate([jax.nn.log_sigmoid(pre) / GATE_TAU, jnp.broadcast_to(lgam_ref[...], (c, QK_L))], axis=1)
    row = lax.broadcasted_iota(jnp.int32, (c, c), 0)
    col = lax.broadcasted_iota(jnp.int32, (c, c), 1)
    causal = col <= row
    b = _nn(causal.astype(F32), la, precision=HIGHEST)
    mid = b[c // 2 - 1:c // 2, :]
    last = b[c - 1:c, :]
    qt = q * jnp.exp(b - mid)
    kt = k * jnp.exp(mid - b)
    qi = (q * jnp.exp(b)).astype(BF16)
    kh = (k * jnp.exp(last - b)).astype(BF16)
    ones = jnp.ones((c, LANE), F32)

    head_qk = lax.broadcasted_iota(jnp.int32, (c, QK_L), 1) // DK_L
    head_v = lax.broadcasted_iota(jnp.int32, (c, V_L), 1) // DV_L
    blockdiag = (lax.broadcasted_iota(jnp.int32, (QK_L, V_L), 0) // DK_L
                 == lax.broadcasted_iota(jnp.int32, (QK_L, V_L), 1) // DV_L)
    causal4 = (lax.broadcasted_iota(jnp.int32, (H_L * c, c), 1)
               <= lax.broadcasted_iota(jnp.int32, (H_L * c, c), 0) % c)
    outs = []
    for mix in range(2):
        sl = slice(mix * QK_L, (mix + 1) * QK_L)
        vm = v[:, mix * V_L:(mix + 1) * V_L].astype(BF16)
        qstack = jnp.concatenate([jnp.where(head_qk == h, qt[:, sl], 0.0) for h in range(H_L)], axis=0)
        att = _nt(qstack.astype(BF16), kt[:, sl].astype(BF16))
        att = jnp.where(causal4, att, 0.0).astype(BF16)
        r = _nn(att, vm)
        o = jnp.zeros((c, V_L), F32)
        for h in range(H_L):
            o = o + jnp.where(head_v == h, r[h * c:(h + 1) * c, :], 0.0)
        s = s_scr[mix]
        o = o + _nn(qi[:, sl], s.astype(BF16))
        decay = jnp.exp(_tn(la[:, sl], ones, precision=HIGHEST))
        kv = _tn(kh[:, sl], vm)
        s_scr[mix] = jnp.concatenate([decay, decay], axis=1) * s + jnp.where(blockdiag, kv, 0.0)
        outs.append(o)

    seg = (lax.broadcasted_iota(jnp.int32, (V_L, V_L), 0) // DV_L
           == lax.broadcasted_iota(jnp.int32, (V_L, V_L), 1) // DV_L).astype(F32) * (1.0 / DV_L)
    ob = outs[0]
    ob = ob * lax.rsqrt(_nn(ob * ob, seg, precision=HIGHEST) + NORM_EPS) * ggla_ref[...]
    ob = ob * jax.nn.silu(gates[:, 0:V_L])
    oc = outs[1]
    dev = oc - _nn(oc, seg, precision=HIGHEST)
    oc = dev * lax.rsqrt(_nn(dev * dev, seg, precision=HIGHEST) + NORM_EPS) * jax.nn.silu(gates[:, V_L:2 * V_L])
    obc_ref[...] = jnp.concatenate([ob, oc], axis=1)

    @pl.when(j == pl.num_programs(1) - 1)
    def _():
        sfin_ref[...] = s_scr[...]


def _linattn_prompt(zl, wg, bg, lgam, ggla, batch, seq):
    nchunk = seq // CHUNK
    obc, sfin = pl.pallas_call(
        _linattn_prompt_kernel,
        grid=(batch, nchunk),
        in_specs=[
            pl.BlockSpec((CHUNK, W_LIN), lambda b, j: (b * nchunk + j, 0)),
            pl.BlockSpec((LANE, QK_L), lambda b, j: (0, 0)),
            pl.BlockSpec((1, QK_L), lambda b, j: (0, 0)),
            pl.BlockSpec((1, QK_L), lambda b, j: (0, 0)),
            pl.BlockSpec((1, V_L), lambda b, j: (0, 0)),
        ],
        out_specs=[
            pl.BlockSpec((CHUNK, 2 * V_L), lambda b, j: (b * nchunk + j, 0)),
            pl.BlockSpec((None, 2, QK_L, V_L), lambda b, j: (b, 0, 0, 0)),
        ],
        out_shape=[jax.ShapeDtypeStruct((batch * seq, 2 * V_L), F32),
                   jax.ShapeDtypeStruct((batch, 2, QK_L, V_L), F32)],
        scratch_shapes=[pltpu.VMEM((2, QK_L, V_L), F32)],
        compiler_params=_params("parallel", "arbitrary"),
        name="linattn_prompt",
    )(zl, wg, bg, lgam, ggla)
    sfin = sfin.reshape(batch, 2, H_L, DK_L, H_L, DV_L)
    return obc, jnp.stack([sfin[:, :, h, :, h, :] for h in range(H_L)], axis=2)


def _linattn_sample_kernel(q_ref, k_ref, gb_ref, v_ref, gate_ref, s0_ref, wgt_ref, bg_ref, lgam_ref, ggla_ref,
                           o_ref, s1_ref, *, n_seq, t_new):
    mix = pl.program_id(0)
    pre = _nn(wgt_ref[...], gb_ref[...], precision=HIGHEST) + bg_ref[...]
    la_gla = jax.nn.log_sigmoid(pre) / GATE_TAU
    la = jnp.where(mix == 0, la_gla, jnp.broadcast_to(lgam_ref[...], la_gla.shape))
    a = jnp.exp(la)
    q = q_ref[...] * (DK_L ** -0.5)
    k = k_ref[...]
    v = v_ref[...]
    s0t = s0_ref[...].T
    s = [s0t[d * DV_L:(d + 1) * DV_L, :] for d in range(DK_L)]
    outs = []
    for t in range(t_new):
        tok = slice(t * n_seq, (t + 1) * n_seq)
        vt = v[:, tok]
        ot = jnp.zeros((DV_L, n_seq), F32)
        for d in range(DK_L):
            s[d] = a[d:d + 1, tok] * s[d] + k[d:d + 1, tok] * vt
            ot = ot + q[d:d + 1, tok] * s[d]
        outs.append(ot)
    s1_ref[...] = jnp.concatenate(s, axis=0).T
    o = jnp.concatenate(outs, axis=1)
    mean_sq = jnp.mean(o * o, axis=0, keepdims=True)
    o_gla = o * lax.rsqrt(mean_sq + NORM_EPS) * ggla_ref[...]
    dev = o - jnp.mean(o, axis=0, keepdims=True)
    o_ret = dev * lax.rsqrt(jnp.mean(dev * dev, axis=0, keepdims=True) + NORM_EPS)
    o_ref[...] = jnp.where(mix == 0, o_gla, o_ret) * jax.nn.silu(gate_ref[...])


def _linattn_sample(zlt, s0, wgt, bg_col, lgam_col, ggla_col, n_seq, t_new):
    ntok = t_new * n_seq
    q0, k0, g0, v0, r0 = 0, 2 * QK_L, 4 * QK_L, 4 * QK_L + LANE, 4 * QK_L + LANE + 2 * V_L
    o, s1 = pl.pallas_call(
        functools.partial(_linattn_sample_kernel, n_seq=n_seq, t_new=t_new),
        grid=(2, H_L),
        in_specs=[
            pl.BlockSpec((DK_L, ntok), lambda m, h: (q0 // DK_L + m * H_L + h, 0)),
            pl.BlockSpec((DK_L, ntok), lambda m, h: (k0 // DK_L + m * H_L + h, 0)),
            pl.BlockSpec((LANE, ntok), lambda m, h: (g0 // LANE, 0)),
            pl.BlockSpec((DV_L, ntok), lambda m, h: (v0 // DV_L + m * H_L + h, 0)),
            pl.BlockSpec((DV_L, ntok), lambda m, h: (r0 // DV_L + m * H_L + h, 0)),
            pl.BlockSpec((n_seq, DK_L * DV_L), lambda m, h: (0, m * H_L + h)),
            pl.BlockSpec((DK_L, LANE), lambda m, h: (h, 0)),
            pl.BlockSpec((DK_L, 1), lambda m, h: (h, 0)),
            pl.BlockSpec((DK_L, 1), lambda m, h: (h, 0)),
            pl.BlockSpec((DV_L, 1), lambda m, h: (0, 0)),
        ],
        out_specs=[
            pl.BlockSpec((DV_L, ntok), lambda m, h: (m * H_L + h, 0)),
            pl.BlockSpec((n_seq, DK_L * DV_L), lambda m, h: (0, m * H_L + h)),
        ],
        out_shape=[jax.ShapeDtypeStruct((2 * V_L, ntok), F32),
                   jax.ShapeDtypeStruct((n_seq, 2 * H_L * DK_L * DV_L), F32)],
        compiler_params=_params("parallel", "parallel"),
        name="linattn_sample",
    )(zlt, zlt, zlt, zlt, zlt, s0, wgt, bg_col, lgam_col, ggla_col)
    return o, s1


def _out_proj_prompt_kernel(o1_ref, o2_ref, o3_ref, l1_ref, l2_ref, l3_ref, obc_ref, x_ref, w_ref, y_ref):
    l1, l2, l3 = l1_ref[...], l2_ref[...], l3_ref[...]
    m = jnp.maximum(jnp.maximum(l1, l2), l3)
    e1, e2, e3 = jnp.exp(l1 - m), jnp.exp(l2 - m), jnp.exp(l3 - m)
    oa = (e1 * o1_ref[...] + e2 * o2_ref[...] + e3 * o3_ref[...]) / (e1 + e2 + e3)
    y = _nn(oa.astype(BF16), w_ref[0:W_A, :]) + _nn(obc_ref[...].astype(BF16), w_ref[W_A:, :])
    y_ref[...] = x_ref[...] + y


def _out_proj_prompt(branches, obc, x, w, tm=512):
    t = x.shape[0]
    half = pl.BlockSpec((tm, W_A), lambda i: (i, 0))
    full = pl.BlockSpec((tm, D_MODEL), lambda i: (i, 0))
    (o1, l1), (o2, l2), (o3, l3) = branches
    return pl.pallas_call(
        _out_proj_prompt_kernel,
        grid=(t // tm,),
        in_specs=[half] * 7 + [full, pl.BlockSpec((D_MODEL, D_MODEL), lambda i: (0, 0))],
        out_specs=full,
        out_shape=jax.ShapeDtypeStruct((t, D_MODEL), F32),
        compiler_params=_params("parallel"),
        name="out_proj_prompt",
    )(o1, o2, o3, l1, l2, l3, obc, x, w)


def _out_proj_sample_kernel(oa_ref, obct_ref, x_ref, w_ref, y_ref):
    y = _nn(oa_ref[...].astype(BF16), w_ref[0:W_A, :]) + _tn(obct_ref[...].astype(BF16), w_ref[W_A:, :])
    y_ref[...] = x_ref[...] + y


def _out_proj_sample(oa, obct, x, w):
    return pl.pallas_call(
        _out_proj_sample_kernel,
        out_shape=jax.ShapeDtypeStruct(x.shape, F32),
        compiler_params=pltpu.CompilerParams(vmem_limit_bytes=VMEM_LIMIT),
        name="out_proj_sample",
    )(oa, obct, x, w)


def _bitonic_sort_desc(x):
    x = list(x)
    n = len(x)
    k = 2
    while k <= n:
        j = k // 2
        while j >= 1:
            for i in range(n):
                l = i ^ j
                if l > i:
                    hi, lo = jnp.maximum(x[i], x[l]), jnp.minimum(x[i], x[l])
                    x[i], x[l] = (hi, lo) if (i & k) == 0 else (lo, hi)
            j //= 2
        k *= 2
    return x


def _merge_top(a, b):
    n = len(a)
    x = [jnp.maximum(a[i], b[n - 1 - i]) for i in range(n)]
    j = n // 2
    while j >= 1:
        for i in range(n):
            l = i ^ j
            if l > i:
                x[i], x[l] = jnp.maximum(x[i], x[l]), jnp.minimum(x[i], x[l])
        j //= 2
    return x


def _top16_rows(s):
    t = s.shape[1]
    slabs = s.reshape(N_KEYS // SUBLANE, SUBLANE, t)
    x = _bitonic_sort_desc([slabs[i] for i in range(N_KEYS // SUBLANE)])
    for shift in (4, 2, 1):
        x = _merge_top(x, [pltpu.roll(xi, shift, 0) for xi in x])
    return x


def _top16_pair_sums(t1, t2):
    k = PEER_TOPK
    cand = [[t1[a] + t2[b] for b in range(k // (a + 1))] for a in range(k)]
    neg = jnp.full_like(t1[0], -jnp.inf)
    g0 = cand[0]
    g1 = _bitonic_sort_desc(cand[1] + cand[2] + cand[3][0:3])
    g2 = _bitonic_sort_desc(cand[3][3:4] + cand[4] + cand[5] + cand[6] + cand[7] + [cand[a][0] for a in range(8, 14)])
    g3 = [jnp.maximum(cand[14][0], cand[15][0]), jnp.minimum(cand[14][0], cand[15][0])] + [neg] * (k - 2)
    return _merge_top(_merge_top(g0, g1), _merge_top(g2, g3))


def _peer_kernel(x_ref, g_ref, wpqt_ref, keys_ref, u_ref, vt_ref, gfin_ref, y_ref,
                 hn_scr, qk_scr, s2_scr, f2_scr, th_scr, f1_scr, acc_scr, act_scr, *, final_norm):
    e = pl.program_id(1)
    tt = hn_scr.shape[0]
    e_blk = u_ref.shape[0]
    nslab = N_KEYS // SUBLANE

    @pl.when(e == 0)
    def _route():
        hn = _rms(x_ref[...], g_ref[...]).astype(BF16)
        hn_scr[...] = hn
        qk_scr[...] = _nt(wpqt_ref[...], hn).reshape(PEER_HEADS * 2, N_KEYS, tt)
        acc_scr[...] = jnp.zeros_like(acc_scr)

        def head(h, carry):
            s1 = _nn(keys_ref[0], qk_scr[2 * h], precision=HIGHEST)
            s2 = _nn(keys_ref[1], qk_scr[2 * h + 1], precision=HIGHEST)
            t1 = _top16_rows(s1)
            t2 = _top16_rows(s2)
            top = _top16_pair_sums(t1, t2)
            tau, best = top[PEER_TOPK - 1], top[0]
            z = jnp.exp(top[0] - best)
            for r in range(1, PEER_TOPK):
                z = z + jnp.exp(top[r] - best)
            s1s = s1.reshape(nslab, SUBLANE, tt)
            s2s = s2.reshape(nslab, SUBLANE, tt)
            th = jnp.full((nslab, SUBLANE, tt), jnp.inf, F32)
            for r in range(PEER_TOPK):
                th = jnp.where(s1s + t2[r][None] >= tau[None], t2[r][None], th)
            th_scr[h] = th.reshape(N_KEYS, tt)
            f1_scr[h] = jnp.exp(s1s - t1[0][None]).reshape(N_KEYS, tt)
            s2_scr[h] = s2
            f2_scr[h] = (jnp.exp(s2s - t2[0][None]) / z[None]).reshape(N_KEYS, tt)
            return carry

        lax.fori_loop(0, PEER_HEADS, head, 0)

    pre = _nt(u_ref[...], hn_scr[...])
    for al in range(e_blk // N_KEYS):
        a = e * (e_blk // N_KEYS) + al
        rows = slice(al * N_KEYS, (al + 1) * N_KEYS)
        gate = jnp.zeros((N_KEYS, tt), F32)
        for h in range(PEER_HEADS):
            th_row = th_scr[h, pl.ds(a, 1), :]
            f1_row = f1_scr[h, pl.ds(a, 1), :]
            gate = gate + jnp.where(s2_scr[h] >= th_row, f2_scr[h], 0.0) * f1_row
        p = pre[rows, :]
        act = 0.5 * p * (1.0 + lax.erf(p * (2.0 ** -0.5))) * gate
        act_scr[rows, :] = act.astype(BF16)
    acc_scr[...] += _nn(vt_ref[...], act_scr[...])

    @pl.when(e == pl.num_programs(1) - 1)
    def _finish():
        y = x_ref[...] + acc_scr[...].T
        if final_norm:
            y = _rms(y, gfin_ref[...])
        y_ref[...] = y


def _peer(x, g, wpqt, keys, u, vt, gfin, final_norm, tt, e_blk=512):
    t = x.shape[0]
    return pl.pallas_call(
        functools.partial(_peer_kernel, final_norm=final_norm),
        grid=(t // tt, N_EXPERTS // e_blk),
        in_specs=[
            pl.BlockSpec((tt, D_MODEL), lambda i, e: (i, 0)),
            pl.BlockSpec((1, D_MODEL), lambda i, e: (0, 0)),
            pl.BlockSpec((2 * PEER_HEADS * N_KEYS, D_MODEL), lambda i, e: (0, 0)),
            pl.BlockSpec((2, N_KEYS, N_KEYS), lambda i, e: (0, 0, 0)),
            pl.BlockSpec((e_blk, D_MODEL), lambda i, e: (e, 0)),
            pl.BlockSpec((D_MODEL, e_blk), lambda i, e: (0, e)),
            pl.BlockSpec((1, D_MODEL), lambda i, e: (0, 0)),
        ],
        out_specs=pl.BlockSpec((tt, D_MODEL), lambda i, e: (i, 0)),
        out_shape=jax.ShapeDtypeStruct((t, D_MODEL), F32),
        scratch_shapes=[
            pltpu.VMEM((tt, D_MODEL), BF16),
            pltpu.VMEM((2 * PEER_HEADS, N_KEYS, tt), F32),
            pltpu.VMEM((PEER_HEADS, N_KEYS, tt), F32),
            pltpu.VMEM((PEER_HEADS, N_KEYS, tt), F32),
            pltpu.VMEM((PEER_HEADS, N_KEYS, tt), F32),
            pltpu.VMEM((PEER_HEADS, N_KEYS, tt), F32),
            pltpu.VMEM((D_MODEL, tt), F32),
            pltpu.VMEM((e_blk, tt), BF16),
        ],
        compiler_params=_params("parallel", "arbitrary"),
        name="peer",
    )(x, g, wpqt, keys, u, vt, gfin)


def _split_w_in(w):
    sizes = (W_A, W_A, W_A, QK_L, QK_L, V_L, V_L, GATE_RANK, QK_L, QK_L, V_L, V_L)
    out, start = [], 0
    for n in sizes:
        out.append(w[:, start:start + n])
        start += n
    return out


def _layer_weights(w_in, w_gate2, b_gate, g_gla, w_out, w_pq, u_tab, v_tab):
    qa, ka, va, qb, kb, vb, rb, gb, qc, kc, vc, gc = _split_w_in(w_in)
    gb = jnp.pad(gb, ((0, 0), (0, LANE - GATE_RANK)))
    wa = jnp.concatenate([qa, ka, va], axis=1).astype(BF16)
    wl = jnp.concatenate([qb, qc, kb, kc, gb, vb, vc, rb, gc], axis=1).astype(BF16)
    wg = jnp.pad(w_gate2, ((0, LANE - GATE_RANK), (0, 0)))
    log_gamma = jnp.log(1.0 - 2.0 ** (-5.0 - jnp.arange(H_L, dtype=F32)))
    lgam = jnp.repeat(log_gamma, DK_L)
    return dict(
        wa=wa, wl=wl, wlt=wl.T, wg=wg, wgt=wg.T,
        bg=b_gate.reshape(1, QK_L), bg_col=b_gate.reshape(QK_L, 1),
        lgam=lgam.reshape(1, QK_L), lgam_col=lgam.reshape(QK_L, 1),
        ggla=jnp.tile(g_gla, H_L).reshape(1, V_L), ggla_col=g_gla.reshape(DV_L, 1),
        w_out=w_out.astype(BF16), wpqt=w_pq.T.astype(BF16),
        u=u_tab.astype(BF16), vt=v_tab.T.astype(BF16),
    )


def kernel(x_prompt, x_sample, cache_kv_win, state_gla, state_ret, w_in, w_gate2, b_gate, g_gla,
           w_out, g_mix, g_ffn, w_pq, sub_keys, u_tab, v_tab, g_final):
    batch, seq, _ = x_prompt.shape
    n_seq, t_new, _ = x_sample.shape
    depth = w_in.shape[0]
    win = min(MAX_WINDOW, seq)
    xp = x_prompt.reshape(batch * seq, D_MODEL)
    xs = x_sample.transpose(1, 0, 2).reshape(t_new * n_seq, D_MODEL)
    tables = _sample_attn_tables(t_new)
    gfin = g_final.reshape(1, D_MODEL)
    kv_p, kv_s, gla_p, gla_s, ret_p, ret_s = [], [], [], [], [], []
    for l in range(depth):
        w = _layer_weights(w_in[l], w_gate2[l], b_gate[l], g_gla[l], w_out[l], w_pq[l], u_tab[l], v_tab[l])
        gm = g_mix[l].reshape(1, D_MODEL)
        gf = g_ffn[l].reshape(1, D_MODEL)
        last = l == depth - 1

        za, zl = _in_proj(xp, gm, w["wa"], w["wl"])
        branches = [_attn_prompt_branch(za, batch, seq, d) for (_, d) in BRANCHES]
        obc, sfin = _linattn_prompt(zl, w["wg"], w["bg"], w["lgam"], w["ggla"], batch, seq)
        xp = _out_proj_prompt(branches, obc, xp, w["w_out"])
        xp = _peer(xp, gf, w["wpqt"], sub_keys[l], w["u"], w["vt"], gfin, last, tt=512)
        kv = za.reshape(batch, seq, 3, H_A, HEAD_DIM)[:, seq - win:, 1:3]
        kv_p.append(kv)
        gla_p.append(sfin[:, 0])
        ret_p.append(sfin[:, 1])

        za_s, zlt = _in_proj_sample(xs, gm, w["wa"], w["wlt"])
        oa_s = _attn_sample(za_s, cache_kv_win[l], tables, n_seq, t_new)
        s0 = jnp.concatenate([state_gla[l].reshape(n_seq, -1), state_ret[l].reshape(n_seq, -1)], axis=1)
        obct, s1 = _linattn_sample(zlt, s0, w["wgt"], w["bg_col"], w["lgam_col"], w["ggla_col"], n_seq, t_new)
        xs = _out_proj_sample(oa_s, obct, xs, w["w_out"])
        xs = _peer(xs, gf, w["wpqt"], sub_keys[l], w["u"], w["vt"], gfin, last, tt=512)
        kv_new = za_s.reshape(t_new, n_seq, 3, H_A, HEAD_DIM)[:, :, 1:3].transpose(1, 0, 2, 3, 4)
        kv_s.append(kv_new)
        s1 = s1.reshape(n_seq, 2, H_L, DK_L, DV_L)
        gla_s.append(s1[:, 0])
        ret_s.append(s1[:, 1])

    y_prompt = xp.reshape(batch, seq, D_MODEL)
    y_sample = xs.reshape(t_new, n_seq, D_MODEL).transpose(1, 0, 2)
    return (y_prompt, y_sample, jnp.stack(kv_p), jnp.stack(kv_s), jnp.stack(gla_p), jnp.stack(gla_s),
            jnp.stack(ret_p), jnp.stack(ret_s))
```

```python
import functools
import math

import numpy as np
import jax
import jax.numpy as jnp
from jax import lax
from jax.experimental import pallas as pl
from jax.experimental.pallas import tpu as pltpu

F32 = jnp.float32
BF16 = jnp.bfloat16
HIGHEST = lax.Precision.HIGHEST

D_MODEL = 1024
HEAD_DIM = 64
H_A = 8
W_A = H_A * HEAD_DIM
BRANCHES = ((128, 1), (512, 4), (2048, 16))
BAND = 128
MAX_WINDOW = 2048
H_L = 4
DK_L = 32
DV_L = 64
QK_L = H_L * DK_L
V_L = H_L * DV_L
GATE_RANK = 16
GATE_TAU = 16.0
N_KEYS = 128
N_EXPERTS = N_KEYS * N_KEYS
PEER_HEADS = 8
PEER_TOPK = 16
NORM_EPS = 1e-6
W_LIN = 2 * 2 * QK_L + 128 + 2 * V_L + 2 * V_L

LANE = 128
SUBLANE = 8
VMEM_LIMIT = 56 * 1024 * 1024


def _params(*sem):
    return pltpu.CompilerParams(dimension_semantics=sem, vmem_limit_bytes=VMEM_LIMIT)


def _rms(x, g):
    return x * lax.rsqrt(jnp.mean(x * x, axis=-1, keepdims=True) + NORM_EPS) * g


def _nt(a, b, **kw):
    return lax.dot_general(a, b, (((1,), (1,)), ((), ())), preferred_element_type=F32, **kw)


def _tn(a, b, **kw):
    return lax.dot_general(a, b, (((0,), (0,)), ((), ())), preferred_element_type=F32, **kw)


def _nn(a, b, **kw):
    return jnp.dot(a, b, preferred_element_type=F32, **kw)


def _in_proj_kernel(x_ref, g_ref, wa_ref, wl_ref, za_ref, zl_ref):
    h = _rms(x_ref[...], g_ref[...]).astype(BF16)
    za_ref[...] = _nn(h, wa_ref[...])
    zl_ref[...] = _nn(h, wl_ref[...])


def _in_proj(x, g, wa, wl, tm=256):
    t = x.shape[0]
    return pl.pallas_call(
        _in_proj_kernel,
        grid=(t // tm,),
        in_specs=[
            pl.BlockSpec((tm, D_MODEL), lambda i: (i, 0)),
            pl.BlockSpec((1, D_MODEL), lambda i: (0, 0)),
            pl.BlockSpec((D_MODEL, 3 * W_A), lambda i: (0, 0)),
            pl.BlockSpec((D_MODEL, W_LIN), lambda i: (0, 0)),
        ],
        out_specs=[
            pl.BlockSpec((tm, 3 * W_A), lambda i: (i, 0)),
            pl.BlockSpec((tm, W_LIN), lambda i: (i, 0)),
        ],
        out_shape=[jax.ShapeDtypeStruct((t, 3 * W_A), F32), jax.ShapeDtypeStruct((t, W_LIN), F32)],
        compiler_params=_params("parallel"),
        name="in_proj",
    )(x, g, wa, wl)


def _in_proj_sample_kernel(x_ref, g_ref, wa_ref, wlt_ref, za_ref, zlt_ref):
    h = _rms(x_ref[...], g_ref[...]).astype(BF16)
    za_ref[...] = _nn(h, wa_ref[...])
    zlt_ref[...] = _nt(wlt_ref[...], h)


def _in_proj_sample(x, g, wa, wlt):
    t = x.shape[0]
    return pl.pallas_call(
        _in_proj_sample_kernel,
        out_shape=[jax.ShapeDtypeStruct((t, 3 * W_A), F32), jax.ShapeDtypeStruct((W_LIN, t), F32)],
        compiler_params=pltpu.CompilerParams(vmem_limit_bytes=VMEM_LIMIT),
        name="in_proj_sample",
    )(x, g, wa, wlt)


def _alibi_slope(h):
    return 2.0 ** (-8.0 * (h + 1) / H_A)


def _attn_prompt_kernel(q_ref, kc_ref, kp_ref, vc_ref, vp_ref, o_ref, lse_ref, *, dilation):
    i = pl.program_id(2)
    qi = lax.broadcasted_iota(jnp.int32, (BAND, 2 * BAND), 0)
    ki = lax.broadcasted_iota(jnp.int32, (BAND, 2 * BAND), 1)
    dist = qi + BAND - ki
    first_key = jnp.where(i > 0, 0, BAND)
    valid = (dist >= 0) & (dist <= BAND) & (ki >= first_key)
    distf = dist.astype(F32) * float(dilation)
    low = lax.broadcasted_iota(jnp.int32, (BAND, LANE), 1) < HEAD_DIM
    for p in range(H_A // 2):
        sl = slice(p * LANE, (p + 1) * LANE)
        q = q_ref[:, sl]
        k = jnp.concatenate([kp_ref[:, sl], kc_ref[:, sl]], axis=0).astype(BF16)
        v = jnp.concatenate([vp_ref[:, sl], vc_ref[:, sl]], axis=0).astype(BF16)
        outs, lses = [], []
        for sub in range(2):
            slope = _alibi_slope(2 * p + sub)
            qm = jnp.where(low if sub == 0 else jnp.logical_not(low), q, 0.0).astype(BF16)
            s = _nt(qm, k) * (HEAD_DIM ** -0.5)
            s = jnp.where(valid, s - slope * distf, -jnp.inf)
            m = jnp.max(s, axis=-1, keepdims=True)
            e = jnp.exp(s - m)
            den = jnp.sum(e, axis=-1, keepdims=True)
            outs.append(_nn(e.astype(BF16), v) / den)
            lses.append(jnp.broadcast_to(m + jnp.log(den), (BAND, LANE)))
        o_ref[:, sl] = jnp.where(low, outs[0], outs[1])
        lse_ref[:, sl] = jnp.where(low, lses[0], lses[1])


def _attn_prompt_branch(za, batch, seq, dilation):
    sub_len = seq // dilation
    nblk = sub_len // BAND
    zav = za.reshape(batch, sub_len, dilation * 3 * W_A)
    blk = (None, BAND, W_A)
    prev = lambda i: jnp.maximum(i - 1, 0)
    o, lse = pl.pallas_call(
        functools.partial(_attn_prompt_kernel, dilation=dilation),
        grid=(batch, dilation, nblk),
        in_specs=[
            pl.BlockSpec(blk, lambda b, r, i: (b, i, 3 * r)),
            pl.BlockSpec(blk, lambda b, r, i: (b, i, 3 * r + 1)),
            pl.BlockSpec(blk, lambda b, r, i: (b, prev(i), 3 * r + 1)),
            pl.BlockSpec(blk, lambda b, r, i: (b, i, 3 * r + 2)),
            pl.BlockSpec(blk, lambda b, r, i: (b, prev(i), 3 * r + 2)),
        ],
        out_specs=[
            pl.BlockSpec(blk, lambda b, r, i: (b, i, r)),
            pl.BlockSpec(blk, lambda b, r, i: (b, i, r)),
        ],
        out_shape=[jax.ShapeDtypeStruct((batch, sub_len, dilation * W_A), F32)] * 2,
        compiler_params=_params("parallel", "parallel", "arbitrary"),
        name=f"attn_prompt_d{dilation}",
    )(zav, zav, zav, zav, zav)
    return o.reshape(batch * seq, W_A), lse.reshape(batch * seq, W_A)


STRIDED_GROUPS = 96
TAIL_ROWS = 512


def _sample_attn_tables(t_new):
    n_past = MAX_WINDOW

    def entry(h, t, idx):
        if t >= t_new:
            return 0.0
        delta = n_past + t - idx
        if delta < 0:
            return -np.inf
        cnt = sum(1 for (w, d) in BRANCHES if delta % d == 0 and delta <= w)
        if cnt == 0:
            return -np.inf
        return -_alibi_slope(h) * delta + math.log(cnt)

    def table(idxs):
        tab = np.zeros((len(idxs), H_A * SUBLANE), np.float32)
        for r, idx in enumerate(idxs):
            for h in range(H_A):
                for t in range(SUBLANE):
                    if idx is None:
                        tab[r, h * SUBLANE + t] = 0.0 if t >= t_new else -np.inf
                    else:
                        tab[r, h * SUBLANE + t] = entry(h, t, idx)
        return tab

    strided = np.stack([table([16 * i + u for i in range(STRIDED_GROUPS)]) for u in range(4)])
    tail = table(list(range(MAX_WINDOW - TAIL_ROWS, MAX_WINDOW)))
    new = table([n_past + t if t < t_new else None for t in range(SUBLANE)])
    return jnp.asarray(strided), jnp.asarray(tail), jnp.asarray(new)


def _heads_to_lanes(ref, lead):
    return jnp.concatenate([ref[lead + (h, slice(None))] for h in range(H_A)], axis=-1)


def _attn_sample_kernel(za_ref, ks_ref, vs_ref, kt_ref, vt_ref, bs_ref, bt_ref, bn_ref, o_ref, *, t_new):
    zero_rows = jnp.zeros((SUBLANE - t_new, W_A), F32)
    rows = [za_ref[t] for t in range(t_new)]
    q8 = jnp.concatenate([r[:, 0:W_A] for r in rows] + [zero_rows], axis=0)
    k8 = jnp.concatenate([r[:, W_A:2 * W_A] for r in rows] + [zero_rows], axis=0)
    v8 = jnp.concatenate([r[:, 2 * W_A:3 * W_A] for r in rows] + [zero_rows], axis=0)
    head_of_lane = lax.broadcasted_iota(jnp.int32, (SUBLANE, W_A), 1) // HEAD_DIM
    qbd = jnp.concatenate([jnp.where(head_of_lane == h, q8, 0.0) for h in range(H_A)], axis=0).astype(BF16)

    pieces = []
    for u in range(4):
        lead = (slice(None), u)
        pieces.append((_heads_to_lanes(ks_ref, lead), _heads_to_lanes(vs_ref, lead), bs_ref[u]))
    pieces.append((_heads_to_lanes(kt_ref, (slice(None),)), _heads_to_lanes(vt_ref, (slice(None),)), bt_ref[...]))
    pieces.append((k8, v8, bn_ref[...]))

    scores = [_nt(k.astype(BF16), qbd) * (HEAD_DIM ** -0.5) + bias for (k, _, bias) in pieces]
    m = scores[0].max(axis=0, keepdims=True)
    for s in scores[1:]:
        m = jnp.maximum(m, s.max(axis=0, keepdims=True))
    acc = jnp.zeros((H_A * SUBLANE, W_A), F32)
    den = jnp.zeros((H_A * SUBLANE, LANE), F32)
    for s, (_, v, _) in zip(scores, pieces):
        p = jnp.exp(s - m).astype(BF16)
        acc = acc + _tn(p, v.astype(BF16))
        den = den + _tn(p, jnp.ones((p.shape[0], LANE), BF16))
    acc = acc / den[:, 0:1]
    out = jnp.zeros((SUBLANE, W_A), F32)
    for h in range(H_A):
        out = out + jnp.where(head_of_lane == h, acc[h * SUBLANE:(h + 1) * SUBLANE, :], 0.0)
    for t in range(t_new):
        o_ref[t] = out[t:t + 1, :]


def _attn_sample(za, cache, layer, tables, n_seq, t_new):
    bs, bt, bn = tables
    depth = cache.shape[0]
    zav = za.reshape(t_new, n_seq, 1, 3 * W_A)
    grouped = cache.reshape(depth, n_seq, MAX_WINDOW // 16, 16, 2, H_A, HEAD_DIM)
    strided = lambda kv: pl.BlockSpec((None, None, STRIDED_GROUPS, 4, None, H_A, HEAD_DIM),
                                      lambda b: (layer, b, 0, 0, kv, 0, 0))
    tail = lambda kv: pl.BlockSpec((None, None, TAIL_ROWS, None, H_A, HEAD_DIM),
                                   lambda b: (layer, b, MAX_WINDOW // TAIL_ROWS - 1, kv, 0, 0))
    o = pl.pallas_call(
        functools.partial(_attn_sample_kernel, t_new=t_new),
        grid=(n_seq,),
        in_specs=[
            pl.BlockSpec((t_new, None, 1, 3 * W_A), lambda b: (0, b, 0, 0)),
            strided(0), strided(1), tail(0), tail(1),
            pl.BlockSpec(bs.shape, lambda b: (0, 0, 0)),
            pl.BlockSpec(bt.shape, lambda b: (0, 0)),
            pl.BlockSpec(bn.shape, lambda b: (0, 0)),
        ],
        out_specs=pl.BlockSpec((t_new, None, 1, W_A), lambda b: (0, b, 0, 0)),
        out_shape=jax.ShapeDtypeStruct((t_new, n_seq, 1, W_A), F32),
        compiler_params=_params("parallel"),
        name="attn_sample",
    )(zav, grouped, grouped, cache, cache, bs, bt, bn)
    return o.reshape(t_new * n_seq, W_A)


CHUNK = 128


def _linattn_prompt_kernel(zl_ref, wg_ref, bg_ref, lgam_ref, ggla_ref, obc_ref, sfin_ref, s_scr):
    j = pl.program_id(1)

    @pl.when(j == 0)
    def _():
        s_scr[...] = jnp.zeros_like(s_scr)

    c = CHUNK
    z = zl_ref[...]
    q = z[:, 0:2 * QK_L] * (DK_L ** -0.5)
    k = z[:, 2 * QK_L:4 * QK_L]
    gb = z[:, 4 * QK_L:4 * QK_L + LANE]
    v = z[:, 4 * QK_L + LANE:4 * QK_L + LANE + 2 * V_L]
    gates = z[:, 4 * QK_L + LANE + 2 * V_L:]

    pre = _nn(gb, wg_ref[...], precision=HIGHEST) + bg_ref[...]
    la = jnp.concatenate([jax.nn.log_sigmoid(pre) / GATE_TAU, jnp.broadcast_to(lgam_ref[...], (c, QK_L))], axis=1)
    row = lax.broadcasted_iota(jnp.int32, (c, c), 0)
    col = lax.broadcasted_iota(jnp.int32, (c, c), 1)
    causal = col <= row
    b = _nn(causal.astype(F32), la, precision=HIGHEST)
    mid = b[c // 2 - 1:c // 2, :]
    last = b[c - 1:c, :]
    qt = q * jnp.exp(b - mid)
    kt = k * jnp.exp(mid - b)
    qi = (q * jnp.exp(b)).astype(BF16)
    kh = (k * jnp.exp(last - b)).astype(BF16)
    ones = jnp.ones((c, LANE), F32)

    head_qk = lax.broadcasted_iota(jnp.int32, (c, QK_L), 1) // DK_L
    head_v = lax.broadcasted_iota(jnp.int32, (c, V_L), 1) // DV_L
    blockdiag = (lax.broadcasted_iota(jnp.int32, (QK_L, V_L), 0) // DK_L
                 == lax.broadcasted_iota(jnp.int32, (QK_L, V_L), 1) // DV_L)
    causal4 = (lax.broadcasted_iota(jnp.int32, (H_L * c, c), 1)
               <= lax.broadcasted_iota(jnp.int32, (H_L * c, c), 0) % c)
    outs = []
    for mix in range(2):
        sl = slice(mix * QK_L, (mix + 1) * QK_L)
        vm = v[:, mix * V_L:(mix + 1) * V_L].astype(BF16)
        qstack = jnp.concatenate([jnp.where(head_qk == h, qt[:, sl], 0.0) for h in range(H_L)], axis=0)
        att = _nt(qstack.astype(BF16), kt[:, sl].astype(BF16))
        att = jnp.where(causal4, att, 0.0).astype(BF16)
        r = _nn(att, vm)
        o = jnp.zeros((c, V_L), F32)
        for h in range(H_L):
            o = o + jnp.where(head_v == h, r[h * c:(h + 1) * c, :], 0.0)
        s = s_scr[mix]
        o = o + _nn(qi[:, sl], s.astype(BF16))
        decay = jnp.exp(_tn(la[:, sl], ones, precision=HIGHEST))
        kv = _tn(kh[:, sl], vm)
        s_scr[mix] = jnp.concatenate([decay, decay], axis=1) * s + jnp.where(blockdiag, kv, 0.0)
        outs.append(o)

    seg = (lax.broadcasted_iota(jnp.int32, (V_L, V_L), 0) // DV_L
           == lax.broadcasted_iota(jnp.int32, (V_L, V_L), 1) // DV_L).astype(F32) * (1.0 / DV_L)
    ob = outs[0]
    ob = ob * lax.rsqrt(_nn(ob * ob, seg, precision=HIGHEST) + NORM_EPS) * ggla_ref[...]
    ob = ob * jax.nn.silu(gates[:, 0:V_L])
    oc = outs[1]
    dev = oc - _nn(oc, seg, precision=HIGHEST)
    oc = dev * lax.rsqrt(_nn(dev * dev, seg, precision=HIGHEST) + NORM_EPS) * jax.nn.silu(gates[:, V_L:2 * V_L])
    obc_ref[...] = jnp.concatenate([ob, oc], axis=1)

    @pl.when(j == pl.num_programs(1) - 1)
    def _():
        sfin_ref[...] = s_scr[...]


def _linattn_prompt(zl, wg, bg, lgam, ggla, batch, seq):
    nchunk = seq // CHUNK
    obc, sfin = pl.pallas_call(
        _linattn_prompt_kernel,
        grid=(batch, nchunk),
        in_specs=[
            pl.BlockSpec((CHUNK, W_LIN), lambda b, j: (b * nchunk + j, 0)),
            pl.BlockSpec((LANE, QK_L), lambda b, j: (0, 0)),
            pl.BlockSpec((1, QK_L), lambda b, j: (0, 0)),
            pl.BlockSpec((1, QK_L), lambda b, j: (0, 0)),
            pl.BlockSpec((1, V_L), lambda b, j: (0, 0)),
        ],
        out_specs=[
            pl.BlockSpec((CHUNK, 2 * V_L), lambda b, j: (b * nchunk + j, 0)),
            pl.BlockSpec((None, 2, QK_L, V_L), lambda b, j: (b, 0, 0, 0)),
        ],
        out_shape=[jax.ShapeDtypeStruct((batch * seq, 2 * V_L), F32),
                   jax.ShapeDtypeStruct((batch, 2, QK_L, V_L), F32)],
        scratch_shapes=[pltpu.VMEM((2, QK_L, V_L), F32)],
        compiler_params=_params("parallel", "arbitrary"),
        name="linattn_prompt",
    )(zl, wg, bg, lgam, ggla)
    sfin = sfin.reshape(batch, 2, H_L, DK_L, H_L, DV_L)
    return obc, jnp.stack([sfin[:, :, h, :, h, :] for h in range(H_L)], axis=2)


def _linattn_sample_kernel(q_ref, k_ref, gb_ref, v_ref, gate_ref, s0_ref, wgt_ref, bg_ref, lgam_ref, ggla_ref,
                           o_ref, s1_ref, *, n_seq, t_new):
    mix = pl.program_id(0)
    pre = _nn(wgt_ref[...], gb_ref[...], precision=HIGHEST) + bg_ref[...]
    la_gla = jax.nn.log_sigmoid(pre) / GATE_TAU
    la = jnp.where(mix == 0, la_gla, jnp.broadcast_to(lgam_ref[...], la_gla.shape))
    a = jnp.exp(la)
    q = q_ref[...] * (DK_L ** -0.5)
    k = k_ref[...]
    v = v_ref[...]
    s0t = s0_ref[...].T
    s = [s0t[d * DV_L:(d + 1) * DV_L, :] for d in range(DK_L)]
    outs = []
    for t in range(t_new):
        tok = slice(t * n_seq, (t + 1) * n_seq)
        vt = v[:, tok]
        ot = jnp.zeros((DV_L, n_seq), F32)
        for d in range(DK_L):
            s[d] = a[d:d + 1, tok] * s[d] + k[d:d + 1, tok] * vt
            ot = ot + q[d:d + 1, tok] * s[d]
        outs.append(ot)
    s1_ref[...] = jnp.concatenate(s, axis=0).T
    o = jnp.concatenate(outs, axis=1)
    mean_sq = jnp.mean(o * o, axis=0, keepdims=True)
    o_gla = o * lax.rsqrt(mean_sq + NORM_EPS) * ggla_ref[...]
    dev = o - jnp.mean(o, axis=0, keepdims=True)
    o_ret = dev * lax.rsqrt(jnp.mean(dev * dev, axis=0, keepdims=True) + NORM_EPS)
    o_ref[...] = jnp.where(mix == 0, o_gla, o_ret) * jax.nn.silu(gate_ref[...])


def _linattn_sample(zlt, s0, wgt, bg_col, lgam_col, ggla_col, n_seq, t_new):
    ntok = t_new * n_seq
    q0, k0, g0, v0, r0 = 0, 2 * QK_L, 4 * QK_L, 4 * QK_L + LANE, 4 * QK_L + LANE + 2 * V_L
    o, s1 = pl.pallas_call(
        functools.partial(_linattn_sample_kernel, n_seq=n_seq, t_new=t_new),
        grid=(2, H_L),
        in_specs=[
            pl.BlockSpec((DK_L, ntok), lambda m, h: (q0 // DK_L + m * H_L + h, 0)),
            pl.BlockSpec((DK_L, ntok), lambda m, h: (k0 // DK_L + m * H_L + h, 0)),
            pl.BlockSpec((LANE, ntok), lambda m, h: (g0 // LANE, 0)),
            pl.BlockSpec((DV_L, ntok), lambda m, h: (v0 // DV_L + m * H_L + h, 0)),
            pl.BlockSpec((DV_L, ntok), lambda m, h: (r0 // DV_L + m * H_L + h, 0)),
            pl.BlockSpec((n_seq, DK_L * DV_L), lambda m, h: (0, m * H_L + h)),
            pl.BlockSpec((DK_L, LANE), lambda m, h: (h, 0)),
            pl.BlockSpec((DK_L, 1), lambda m, h: (h, 0)),
            pl.BlockSpec((DK_L, 1), lambda m, h: (h, 0)),
            pl.BlockSpec((DV_L, 1), lambda m, h: (0, 0)),
        ],
        out_specs=[
            pl.BlockSpec((DV_L, ntok), lambda m, h: (m * H_L + h, 0)),
            pl.BlockSpec((n_seq, DK_L * DV_L), lambda m, h: (0, m * H_L + h)),
        ],
        out_shape=[jax.ShapeDtypeStruct((2 * V_L, ntok), F32),
                   jax.ShapeDtypeStruct((n_seq, 2 * H_L * DK_L * DV_L), F32)],
        compiler_params=_params("parallel", "parallel"),
        name="linattn_sample",
    )(zlt, zlt, zlt, zlt, zlt, s0, wgt, bg_col, lgam_col, ggla_col)
    return o, s1


def _out_proj_prompt_kernel(o1_ref, o2_ref, o3_ref, l1_ref, l2_ref, l3_ref, obc_ref, x_ref, w_ref, y_ref):
    l1, l2, l3 = l1_ref[...], l2_ref[...], l3_ref[...]
    m = jnp.maximum(jnp.maximum(l1, l2), l3)
    e1, e2, e3 = jnp.exp(l1 - m), jnp.exp(l2 - m), jnp.exp(l3 - m)
    oa = (e1 * o1_ref[...] + e2 * o2_ref[...] + e3 * o3_ref[...]) / (e1 + e2 + e3)
    y = _nn(oa.astype(BF16), w_ref[0:W_A, :]) + _nn(obc_ref[...].astype(BF16), w_ref[W_A:, :])
    y_ref[...] = x_ref[...] + y


def _out_proj_prompt(branches, obc, x, w, tm=512):
    t = x.shape[0]
    half = pl.BlockSpec((tm, W_A), lambda i: (i, 0))
    full = pl.BlockSpec((tm, D_MODEL), lambda i: (i, 0))
    (o1, l1), (o2, l2), (o3, l3) = branches
    return pl.pallas_call(
        _out_proj_prompt_kernel,
        grid=(t // tm,),
        in_specs=[half] * 7 + [full, pl.BlockSpec((D_MODEL, D_MODEL), lambda i: (0, 0))],
        out_specs=full,
        out_shape=jax.ShapeDtypeStruct((t, D_MODEL), F32),
        compiler_params=_params("parallel"),
        name="out_proj_prompt",
    )(o1, o2, o3, l1, l2, l3, obc, x, w)


def _out_proj_sample_kernel(oa_ref, obct_ref, x_ref, w_ref, y_ref):
    y = _nn(oa_ref[...].astype(BF16), w_ref[0:W_A, :]) + _tn(obct_ref[...].astype(BF16), w_ref[W_A:, :])
    y_ref[...] = x_ref[...] + y


def _out_proj_sample(oa, obct, x, w):
    return pl.pallas_call(
        _out_proj_sample_kernel,
        out_shape=jax.ShapeDtypeStruct(x.shape, F32),
        compiler_params=pltpu.CompilerParams(vmem_limit_bytes=VMEM_LIMIT),
        name="out_proj_sample",
    )(oa, obct, x, w)


def _bitonic_sort_desc(x):
    x = list(x)
    n = len(x)
    k = 2
    while k <= n:
        j = k // 2
        while j >= 1:
            for i in range(n):
                l = i ^ j
                if l > i:
                    hi, lo = jnp.maximum(x[i], x[l]), jnp.minimum(x[i], x[l])
                    x[i], x[l] = (hi, lo) if (i & k) == 0 else (lo, hi)
            j //= 2
        k *= 2
    return x


def _merge_top(a, b):
    n = len(a)
    x = [jnp.maximum(a[i], b[n - 1 - i]) for i in range(n)]
    j = n // 2
    while j >= 1:
        for i in range(n):
            l = i ^ j
            if l > i:
                x[i], x[l] = jnp.maximum(x[i], x[l]), jnp.minimum(x[i], x[l])
        j //= 2
    return x


def _top16_rows(s):
    t = s.shape[1]
    slabs = s.reshape(N_KEYS // SUBLANE, SUBLANE, t)
    x = _bitonic_sort_desc([slabs[i] for i in range(N_KEYS // SUBLANE)])
    for shift in (4, 2, 1):
        x = _merge_top(x, [pltpu.roll(xi, shift, 0) for xi in x])
    return x


def _top16_pair_sums(t1, t2):
    k = PEER_TOPK
    cand = [[t1[a] + t2[b] for b in range(k // (a + 1))] for a in range(k)]
    neg = jnp.full_like(t1[0], -jnp.inf)
    g0 = cand[0]
    g1 = _bitonic_sort_desc(cand[1] + cand[2] + cand[3][0:3])
    g2 = _bitonic_sort_desc(cand[3][3:4] + cand[4] + cand[5] + cand[6] + cand[7] + [cand[a][0] for a in range(8, 14)])
    g3 = [jnp.maximum(cand[14][0], cand[15][0]), jnp.minimum(cand[14][0], cand[15][0])] + [neg] * (k - 2)
    return _merge_top(_merge_top(g0, g1), _merge_top(g2, g3))


PEER_GROUP = 4


def _peer_kernel(x_ref, g_ref, wpqt_ref, keys_ref, u_ref, vt_ref, gfin_ref, y_ref,
                 hn_scr, rank_scr, f2_scr, nsel_scr, f1_scr, acc_scr, pre_scr, act_scr, *, final_norm):
    e = pl.program_id(1)
    tt = hn_scr.shape[0]
    e_blk = u_ref.shape[0]
    nslab = N_KEYS // SUBLANE
    grp_rows = PEER_GROUP * N_KEYS

    @pl.when(e == 0)
    def _route():
        hn_scr[...] = _rms(x_ref[...], g_ref[...]).astype(BF16)
        acc_scr[...] = jnp.zeros_like(acc_scr)

        def head(h, carry):
            w_rows = pl.ds(pl.multiple_of(h * 2 * N_KEYS, 2 * N_KEYS), 2 * N_KEYS)
            qk = _nt(wpqt_ref[w_rows, :], hn_scr[...])
            s1 = _nn(keys_ref[0], qk[0:N_KEYS], precision=HIGHEST)
            s2 = _nn(keys_ref[1], qk[N_KEYS:2 * N_KEYS], precision=HIGHEST)
            t1 = _top16_rows(s1)
            t2 = _top16_rows(s2)
            top = _top16_pair_sums(t1, t2)
            tau, best = top[PEER_TOPK - 1], top[0]
            z = jnp.exp(top[0] - best)
            for r in range(1, PEER_TOPK):
                z = z + jnp.exp(top[r] - best)
            s1s = s1.reshape(nslab, SUBLANE, tt)
            s2s = s2.reshape(nslab, SUBLANE, tt)
            nsel = jnp.zeros((nslab, SUBLANE, tt), F32)
            rank = jnp.zeros((nslab, SUBLANE, tt), F32)
            for r in range(PEER_TOPK):
                nsel = nsel + jnp.where(s1s + t2[r][None] >= tau[None], 1.0, 0.0)
                rank = rank + jnp.where(t2[r][None] > s2s, 1.0, 0.0)
            nsel_scr[h] = nsel.reshape(N_KEYS, tt)
            rank_scr[h] = rank.reshape(N_KEYS, tt).astype(BF16)
            f1_scr[h] = jnp.exp(s1s - t1[0][None]).reshape(N_KEYS, tt)
            f2_scr[h] = (jnp.exp(s2s - t2[0][None]) / z[None]).reshape(N_KEYS, tt).astype(BF16)
            return carry

        lax.fori_loop(0, PEER_HEADS, head, 0)

    def pre_activations(g):
        rows = slice(g * grp_rows, (g + 1) * grp_rows)
        pre_scr[g % 2] = _nt(u_ref[rows, :], hn_scr[...])

    def gated_activations(g):
        for al in range(PEER_GROUP):
            a = e * (e_blk // N_KEYS) + g * PEER_GROUP + al
            rows = slice(al * N_KEYS, (al + 1) * N_KEYS)
            n_rows = [nsel_scr[h, pl.ds(a, 1), :] for h in range(PEER_HEADS)]
            f1_rows = [f1_scr[h, pl.ds(a, 1), :] for h in range(PEER_HEADS)]
            for j in range(tt // LANE):
                ln = slice(j * LANE, (j + 1) * LANE)
                gate = jnp.zeros((N_KEYS, LANE), BF16)
                for h in range(PEER_HEADS):
                    n_row = jnp.broadcast_to(n_rows[h][:, ln], (N_KEYS, LANE)).astype(BF16)
                    f1_row = jnp.broadcast_to(f1_rows[h][:, ln], (N_KEYS, LANE)).astype(BF16)
                    gate = gate + jnp.where(rank_scr[h, :, ln] < n_row, f2_scr[h, :, ln], 0.0) * f1_row
                p = pre_scr[g % 2, rows, ln]
                gelu = 0.5 * p * (1.0 + lax.erf(p * (2.0 ** -0.5)))
                act_scr[g % 2, rows, ln] = gelu.astype(BF16) * gate

    def accumulate(g):
        cols = slice(g * grp_rows, (g + 1) * grp_rows)
        acc_scr[...] += _nn(vt_ref[:, cols], act_scr[g % 2])

    n_grp = e_blk // grp_rows
    pre_activations(0)
    for g in range(n_grp):
        if g + 1 < n_grp:
            pre_activations(g + 1)
        gated_activations(g)
        if g >= 1:
            accumulate(g - 1)
    accumulate(n_grp - 1)

    @pl.when(e == pl.num_programs(1) - 1)
    def _finish():
        y = x_ref[...] + acc_scr[...].T
        if final_norm:
            y = _rms(y, gfin_ref[...])
        y_ref[...] = y


def _peer(x, g, wpqt, keys, u, vt, gfin, final_norm, tt, e_blk=2048):
    t = x.shape[0]
    return pl.pallas_call(
        functools.partial(_peer_kernel, final_norm=final_norm),
        grid=(t // tt, N_EXPERTS // e_blk),
        in_specs=[
            pl.BlockSpec((tt, D_MODEL), lambda i, e: (i, 0)),
            pl.BlockSpec((1, D_MODEL), lambda i, e: (0, 0)),
            pl.BlockSpec((2 * PEER_HEADS * N_KEYS, D_MODEL), lambda i, e: (0, 0)),
            pl.BlockSpec((2, N_KEYS, N_KEYS), lambda i, e: (0, 0, 0)),
            pl.BlockSpec((e_blk, D_MODEL), lambda i, e: (e, 0)),
            pl.BlockSpec((D_MODEL, e_blk), lambda i, e: (0, e)),
            pl.BlockSpec((1, D_MODEL), lambda i, e: (0, 0)),
        ],
        out_specs=pl.BlockSpec((tt, D_MODEL), lambda i, e: (i, 0)),
        out_shape=jax.ShapeDtypeStruct((t, D_MODEL), F32),
        scratch_shapes=[
            pltpu.VMEM((tt, D_MODEL), BF16),
            pltpu.VMEM((PEER_HEADS, N_KEYS, tt), BF16),
            pltpu.VMEM((PEER_HEADS, N_KEYS, tt), BF16),
            pltpu.VMEM((PEER_HEADS, N_KEYS, tt), F32),
            pltpu.VMEM((PEER_HEADS, N_KEYS, tt), F32),
            pltpu.VMEM((D_MODEL, tt), F32),
            pltpu.VMEM((2, PEER_GROUP * N_KEYS, tt), F32),
            pltpu.VMEM((2, PEER_GROUP * N_KEYS, tt), BF16),
        ],
        compiler_params=_params("parallel", "arbitrary"),
        name="peer",
    )(x, g, wpqt, keys, u, vt, gfin)


def _split_w_in(w):
    sizes = (W_A, W_A, W_A, QK_L, QK_L, V_L, V_L, GATE_RANK, QK_L, QK_L, V_L, V_L)
    out, start = [], 0
    for n in sizes:
        out.append(w[:, start:start + n])
        start += n
    return out


def _layer_weights(w_in, w_gate2, b_gate, g_gla, w_out, w_pq, u_tab, v_tab):
    qa, ka, va, qb, kb, vb, rb, gb, qc, kc, vc, gc = _split_w_in(w_in)
    gb = jnp.pad(gb, ((0, 0), (0, LANE - GATE_RANK)))
    wa = jnp.concatenate([qa, ka, va], axis=1).astype(BF16)
    wl = jnp.concatenate([qb, qc, kb, kc, gb, vb, vc, rb, gc], axis=1).astype(BF16)
    wg = jnp.pad(w_gate2, ((0, LANE - GATE_RANK), (0, 0)))
    log_gamma = jnp.log(1.0 - 2.0 ** (-5.0 - jnp.arange(H_L, dtype=F32)))
    lgam = jnp.repeat(log_gamma, DK_L)
    return dict(
        wa=wa, wl=wl, wlt=wl.T, wg=wg, wgt=wg.T,
        bg=b_gate.reshape(1, QK_L), bg_col=b_gate.reshape(QK_L, 1),
        lgam=lgam.reshape(1, QK_L), lgam_col=lgam.reshape(QK_L, 1),
        ggla=jnp.tile(g_gla, H_L).reshape(1, V_L), ggla_col=g_gla.reshape(DV_L, 1),
        w_out=w_out.astype(BF16), wpqt=w_pq.T.astype(BF16),
        u=u_tab.astype(BF16), vt=v_tab.T.astype(BF16),
    )


def kernel(x_prompt, x_sample, cache_kv_win, state_gla, state_ret, w_in, w_gate2, b_gate, g_gla,
           w_out, g_mix, g_ffn, w_pq, sub_keys, u_tab, v_tab, g_final):
    batch, seq, _ = x_prompt.shape
    n_seq, t_new, _ = x_sample.shape
    depth = w_in.shape[0]
    win = min(MAX_WINDOW, seq)
    xp = x_prompt.reshape(batch * seq, D_MODEL)
    xs = x_sample.transpose(1, 0, 2).reshape(t_new * n_seq, D_MODEL)
    tables = _sample_attn_tables(t_new)
    gfin = g_final.reshape(1, D_MODEL)
    kv_p, kv_s, gla_p, gla_s, ret_p, ret_s = [], [], [], [], [], []
    for l in range(depth):
        w = _layer_weights(w_in[l], w_gate2[l], b_gate[l], g_gla[l], w_out[l], w_pq[l], u_tab[l], v_tab[l])
        gm = g_mix[l].reshape(1, D_MODEL)
        gf = g_ffn[l].reshape(1, D_MODEL)
        last = l == depth - 1

        za, zl = _in_proj(xp, gm, w["wa"], w["wl"])
        branches = [_attn_prompt_branch(za, batch, seq, d) for (_, d) in BRANCHES]
        obc, sfin = _linattn_prompt(zl, w["wg"], w["bg"], w["lgam"], w["ggla"], batch, seq)
        xp = _out_proj_prompt(branches, obc, xp, w["w_out"])
        xp = _peer(xp, gf, w["wpqt"], sub_keys[l], w["u"], w["vt"], gfin, last, tt=512)
        kv = za.reshape(batch, seq, 3, H_A, HEAD_DIM)[:, seq - win:, 1:3]
        kv_p.append(kv)
        gla_p.append(sfin[:, 0])
        ret_p.append(sfin[:, 1])

        za_s, zlt = _in_proj_sample(xs, gm, w["wa"], w["wlt"])
        oa_s = _attn_sample(za_s, cache_kv_win, l, tables, n_seq, t_new)
        s0 = jnp.concatenate([state_gla[l].reshape(n_seq, -1), state_ret[l].reshape(n_seq, -1)], axis=1)
        obct, s1 = _linattn_sample(zlt, s0, w["wgt"], w["bg_col"], w["lgam_col"], w["ggla_col"], n_seq, t_new)
        xs = _out_proj_sample(oa_s, obct, xs, w["w_out"])
        xs = _peer(xs, gf, w["wpqt"], sub_keys[l], w["u"], w["vt"], gfin, last, tt=512)
        kv_new = za_s.reshape(t_new, n_seq, 3, H_A, HEAD_DIM)[:, :, 1:3].transpose(1, 0, 2, 3, 4)
        kv_s.append(kv_new)
        s1 = s1.reshape(n_seq, 2, H_L, DK_L, DV_L)
        gla_s.append(s1[:, 0])
        ret_s.append(s1[:, 1])

    y_prompt = xp.reshape(batch, seq, D_MODEL)
    y_sample = xs.reshape(t_new, n_seq, D_MODEL).transpose(1, 0, 2)
    return (y_prompt, y_sample, jnp.stack(kv_p), jnp.stack(kv_s), jnp.stack(gla_p), jnp.stack(gla_s),
            jnp.stack(ret_p), jnp.stack(ret_s))
```

```python
import functools
import math

import numpy as np
import jax
import jax.numpy as jnp
from jax import lax
from jax.experimental import pallas as pl
from jax.experimental.pallas import tpu as pltpu

F32 = jnp.float32
BF16 = jnp.bfloat16
HIGHEST = lax.Precision.HIGHEST

D_MODEL = 1024
HEAD_DIM = 64
H_A = 8
W_A = H_A * HEAD_DIM
BRANCHES = ((128, 1), (512, 4), (2048, 16))
BAND = 128
MAX_WINDOW = 2048
H_L = 4
DK_L = 32
DV_L = 64
QK_L = H_L * DK_L
V_L = H_L * DV_L
GATE_RANK = 16
GATE_TAU = 16.0
N_KEYS = 128
N_EXPERTS = N_KEYS * N_KEYS
PEER_HEADS = 8
PEER_TOPK = 16
NORM_EPS = 1e-6
W_LIN = 2 * 2 * QK_L + 128 + 2 * V_L + 2 * V_L

LANE = 128
SUBLANE = 8
VMEM_PHYSICAL = 64 * 1024 * 1024
VMEM_LIMIT = VMEM_PHYSICAL * 7 // 8


def _params(*sem, vmem=VMEM_LIMIT):
    return pltpu.CompilerParams(dimension_semantics=sem, vmem_limit_bytes=vmem)


def _rms(x, g):
    return x * lax.rsqrt(jnp.mean(x * x, axis=-1, keepdims=True) + NORM_EPS) * g


def _nt(a, b, **kw):
    return lax.dot_general(a, b, (((1,), (1,)), ((), ())), preferred_element_type=F32, **kw)


def _tn(a, b, **kw):
    return lax.dot_general(a, b, (((0,), (0,)), ((), ())), preferred_element_type=F32, **kw)


def _nn(a, b, **kw):
    return jnp.dot(a, b, preferred_element_type=F32, **kw)


def _in_proj_kernel(x_ref, g_ref, wa_ref, wl_ref, za_ref, zl_ref):
    h = _rms(x_ref[...], g_ref[...]).astype(BF16)
    za_ref[...] = _nn(h, wa_ref[...])
    zl_ref[...] = _nn(h, wl_ref[...])


def _in_proj(x, g, wa, wl, tm=256):
    t = x.shape[0]
    return pl.pallas_call(
        _in_proj_kernel,
        grid=(t // tm,),
        in_specs=[
            pl.BlockSpec((tm, D_MODEL), lambda i: (i, 0)),
            pl.BlockSpec((1, D_MODEL), lambda i: (0, 0)),
            pl.BlockSpec((D_MODEL, 3 * W_A), lambda i: (0, 0)),
            pl.BlockSpec((D_MODEL, W_LIN), lambda i: (0, 0)),
        ],
        out_specs=[
            pl.BlockSpec((tm, 3 * W_A), lambda i: (i, 0)),
            pl.BlockSpec((tm, W_LIN), lambda i: (i, 0)),
        ],
        out_shape=[jax.ShapeDtypeStruct((t, 3 * W_A), F32), jax.ShapeDtypeStruct((t, W_LIN), F32)],
        compiler_params=_params("parallel"),
        name="in_proj",
    )(x, g, wa, wl)


def _in_proj_sample_kernel(x_ref, g_ref, wa_ref, wlt_ref, za_ref, zlt_ref):
    h = _rms(x_ref[...], g_ref[...]).astype(BF16)
    za_ref[...] = _nn(h, wa_ref[...])
    zlt_ref[...] = _nt(wlt_ref[...], h)


def _in_proj_sample(x, g, wa, wlt):
    t = x.shape[0]
    return pl.pallas_call(
        _in_proj_sample_kernel,
        out_shape=[jax.ShapeDtypeStruct((t, 3 * W_A), F32), jax.ShapeDtypeStruct((W_LIN, t), F32)],
        compiler_params=pltpu.CompilerParams(vmem_limit_bytes=VMEM_LIMIT),
        name="in_proj_sample",
    )(x, g, wa, wlt)


def _alibi_slope(h):
    return 2.0 ** (-8.0 * (h + 1) / H_A)


def _attn_prompt_kernel(q_ref, kc_ref, kp_ref, vc_ref, vp_ref, o_ref, lse_ref, *, dilation):
    i = pl.program_id(2)
    qi = lax.broadcasted_iota(jnp.int32, (BAND, 2 * BAND), 0)
    ki = lax.broadcasted_iota(jnp.int32, (BAND, 2 * BAND), 1)
    dist = qi + BAND - ki
    first_key = jnp.where(i > 0, 0, BAND)
    valid = (dist >= 0) & (dist <= BAND) & (ki >= first_key)
    distf = dist.astype(F32) * float(dilation)
    low = lax.broadcasted_iota(jnp.int32, (BAND, LANE), 1) < HEAD_DIM
    for p in range(H_A // 2):
        sl = slice(p * LANE, (p + 1) * LANE)
        q = q_ref[:, sl]
        k = jnp.concatenate([kp_ref[:, sl], kc_ref[:, sl]], axis=0).astype(BF16)
        v = jnp.concatenate([vp_ref[:, sl], vc_ref[:, sl]], axis=0).astype(BF16)
        outs, lses = [], []
        for sub in range(2):
            slope = _alibi_slope(2 * p + sub)
            qm = jnp.where(low if sub == 0 else jnp.logical_not(low), q, 0.0).astype(BF16)
            s = _nt(qm, k) * (HEAD_DIM ** -0.5)
            s = jnp.where(valid, s - slope * distf, -jnp.inf)
            m = jnp.max(s, axis=-1, keepdims=True)
            e = jnp.exp(s - m)
            den = jnp.sum(e, axis=-1, keepdims=True)
            outs.append(_nn(e.astype(BF16), v) / den)
            lses.append(jnp.broadcast_to(m + jnp.log(den), (BAND, LANE)))
        o_ref[:, sl] = jnp.where(low, outs[0], outs[1])
        lse_ref[:, sl] = jnp.where(low, lses[0], lses[1])


def _attn_prompt_branch(za, batch, seq, dilation):
    sub_len = seq // dilation
    nblk = sub_len // BAND
    zav = za.reshape(batch, sub_len, dilation * 3 * W_A)
    blk = (None, BAND, W_A)
    prev = lambda i: jnp.maximum(i - 1, 0)
    o, lse = pl.pallas_call(
        functools.partial(_attn_prompt_kernel, dilation=dilation),
        grid=(batch, dilation, nblk),
        in_specs=[
            pl.BlockSpec(blk, lambda b, r, i: (b, i, 3 * r)),
            pl.BlockSpec(blk, lambda b, r, i: (b, i, 3 * r + 1)),
            pl.BlockSpec(blk, lambda b, r, i: (b, prev(i), 3 * r + 1)),
            pl.BlockSpec(blk, lambda b, r, i: (b, i, 3 * r + 2)),
            pl.BlockSpec(blk, lambda b, r, i: (b, prev(i), 3 * r + 2)),
        ],
        out_specs=[
            pl.BlockSpec(blk, lambda b, r, i: (b, i, r)),
            pl.BlockSpec(blk, lambda b, r, i: (b, i, r)),
        ],
        out_shape=[jax.ShapeDtypeStruct((batch, sub_len, dilation * W_A), F32)] * 2,
        compiler_params=_params("parallel", "parallel", "arbitrary"),
        name=f"attn_prompt_d{dilation}",
    )(zav, zav, zav, zav, zav)
    return o.reshape(batch * seq, W_A), lse.reshape(batch * seq, W_A)


def _sample_attn_tables(t_new):
    n_past = MAX_WINDOW

    def entry(h, t, idx):
        delta = n_past + t - idx
        if delta < 0:
            return -np.inf
        cnt = sum(1 for (w, d) in BRANCHES if delta % d == 0 and delta <= w)
        return -_alibi_slope(h) * delta + math.log(cnt) if cnt else -np.inf

    past = np.zeros((H_A, SUBLANE, MAX_WINDOW), np.float32)
    new = np.zeros((H_A, SUBLANE, SUBLANE), np.float32)
    for h in range(H_A):
        for t in range(t_new):
            past[h, t] = [entry(h, t, idx) for idx in range(MAX_WINDOW)]
            new[h, t] = [entry(h, t, n_past + s) if s < t_new else -np.inf for s in range(SUBLANE)]
    return jnp.asarray(past), jnp.asarray(new)


def _attn_sample_kernel(za_ref, kt_ref, vt_ref, bp_ref, bn_ref, o_ref, *, t_new):
    zero_rows = jnp.zeros((SUBLANE - t_new, W_A), F32)
    rows = [za_ref[t] for t in range(t_new)]
    q8 = jnp.concatenate([r[:, 0:W_A] for r in rows] + [zero_rows], axis=0)
    k8 = jnp.concatenate([r[:, W_A:2 * W_A] for r in rows] + [zero_rows], axis=0)
    v8 = jnp.concatenate([r[:, 2 * W_A:3 * W_A] for r in rows] + [zero_rows], axis=0)
    scale = HEAD_DIM ** -0.5
    outs = []
    for h in range(H_A):
        sl = slice(h * HEAD_DIM, (h + 1) * HEAD_DIM)
        qh = q8[:, sl].astype(BF16)
        s_past = _nn(qh, kt_ref[h].astype(BF16)) * scale + bp_ref[h]
        s_new = _nt(qh, k8[:, sl].astype(BF16)) * scale + bn_ref[h]
        m = jnp.maximum(s_past.max(axis=-1, keepdims=True), s_new.max(axis=-1, keepdims=True))
        p_past = jnp.exp(s_past - m)
        p_new = jnp.exp(s_new - m)
        den = p_past.sum(axis=-1, keepdims=True) + p_new.sum(axis=-1, keepdims=True)
        o = _nt(p_past.astype(BF16), vt_ref[h].astype(BF16)) + _nn(p_new.astype(BF16), v8[:, sl].astype(BF16))
        outs.append(o / den)
    out = jnp.concatenate(outs, axis=-1)
    for t in range(t_new):
        o_ref[t] = out[t:t + 1, :]


def _attn_sample(za, cache_t, layer, tables, n_seq, t_new):
    bp, bn = tables
    zav = za.reshape(t_new, n_seq, 1, 3 * W_A)
    window = lambda kv: pl.BlockSpec((None, None, None, H_A, HEAD_DIM, MAX_WINDOW),
                                     lambda b: (layer, b, kv, 0, 0, 0))
    o = pl.pallas_call(
        functools.partial(_attn_sample_kernel, t_new=t_new),
        grid=(n_seq,),
        in_specs=[
            pl.BlockSpec((t_new, None, 1, 3 * W_A), lambda b: (0, b, 0, 0)),
            window(0), window(1),
            pl.BlockSpec(bp.shape, lambda b: (0, 0, 0)),
            pl.BlockSpec(bn.shape, lambda b: (0, 0, 0)),
        ],
        out_specs=pl.BlockSpec((t_new, None, 1, W_A), lambda b: (0, b, 0, 0)),
        out_shape=jax.ShapeDtypeStruct((t_new, n_seq, 1, W_A), F32),
        compiler_params=_params("parallel"),
        name="attn_sample",
    )(zav, cache_t, cache_t, bp, bn)
    return o.reshape(t_new * n_seq, W_A)


CHUNK = 128


def _linattn_prompt_kernel(zl_ref, wg_ref, bg_ref, lgam_ref, ggla_ref, obc_ref, sfin_ref, s_scr):
    j = pl.program_id(1)

    @pl.when(j == 0)
    def _():
        s_scr[...] = jnp.zeros_like(s_scr)

    c = CHUNK
    z = zl_ref[...]
    q = z[:, 0:2 * QK_L] * (DK_L ** -0.5)
    k = z[:, 2 * QK_L:4 * QK_L]
    gb = z[:, 4 * QK_L:4 * QK_L + LANE]
    v = z[:, 4 * QK_L + LANE:4 * QK_L + LANE + 2 * V_L]
    gates = z[:, 4 * QK_L + LANE + 2 * V_L:]

    pre = _nn(gb, wg_ref[...], precision=HIGHEST) + bg_ref[...]
    la = jnp.concatenate([jax.nn.log_sigmoid(pre) / GATE_TAU, jnp.broadcast_to(lgam_ref[...], (c, QK_L))], axis=1)
    row = lax.broadcasted_iota(jnp.int32, (c, c), 0)
    col = lax.broadcasted_iota(jnp.int32, (c, c), 1)
    causal = col <= row
    b = _nn(causal.astype(F32), la, precision=HIGHEST)
    mid = b[c // 2 - 1:c // 2, :]
    last = b[c - 1:c, :]
    qt = q * jnp.exp(b - mid)
    kt = k * jnp.exp(mid - b)
    qi = (q * jnp.exp(b)).astype(BF16)
    kh = (k * jnp.exp(last - b)).astype(BF16)
    ones = jnp.ones((c, LANE), F32)

    head_qk = lax.broadcasted_iota(jnp.int32, (c, QK_L), 1) // DK_L
    head_v = lax.broadcasted_iota(jnp.int32, (c, V_L), 1) // DV_L
    blockdiag = (lax.broadcasted_iota(jnp.int32, (QK_L, V_L), 0) // DK_L
                 == lax.broadcasted_iota(jnp.int32, (QK_L, V_L), 1) // DV_L)
    causal4 = (lax.broadcasted_iota(jnp.int32, (H_L * c, c), 1)
               <= lax.broadcasted_iota(jnp.int32, (H_L * c, c), 0) % c)
    outs = []
    for mix in range(2):
        sl = slice(mix * QK_L, (mix + 1) * QK_L)
        vm = v[:, mix * V_L:(mix + 1) * V_L].astype(BF16)
        qstack = jnp.concatenate([jnp.where(head_qk == h, qt[:, sl], 0.0) for h in range(H_L)], axis=0)
        att = _nt(qstack.astype(BF16), kt[:, sl].astype(BF16))
        att = jnp.where(causal4, att, 0.0).astype(BF16)
        r = _nn(att, vm)
        o = jnp.zeros((c, V_L), F32)
        for h in range(H_L):
            o = o + jnp.where(head_v == h, r[h * c:(h + 1) * c, :], 0.0)
        s = s_scr[mix]
        o = o + _nn(qi[:, sl], s.astype(BF16))
        decay = jnp.exp(_tn(la[:, sl], ones, precision=HIGHEST))
        kv = _tn(kh[:, sl], vm)
        s_scr[mix] = jnp.concatenate([decay, decay], axis=1) * s + jnp.where(blockdiag, kv, 0.0)
        outs.append(o)

    seg = (lax.broadcasted_iota(jnp.int32, (V_L, V_L), 0) // DV_L
           == lax.broadcasted_iota(jnp.int32, (V_L, V_L), 1) // DV_L).astype(F32) * (1.0 / DV_L)
    ob = outs[0]
    ob = ob * lax.rsqrt(_nn(ob * ob, seg, precision=HIGHEST) + NORM_EPS) * ggla_ref[...]
    ob = ob * jax.nn.silu(gates[:, 0:V_L])
    oc = outs[1]
    dev = oc - _nn(oc, seg, precision=HIGHEST)
    oc = dev * lax.rsqrt(_nn(dev * dev, seg, precision=HIGHEST) + NORM_EPS) * jax.nn.silu(gates[:, V_L:2 * V_L])
    obc_ref[...] = jnp.concatenate([ob, oc], axis=1)

    @pl.when(j == pl.num_programs(1) - 1)
    def _():
        sfin_ref[...] = s_scr[...]


def _linattn_prompt(zl, wg, bg, lgam, ggla, batch, seq):
    nchunk = seq // CHUNK
    obc, sfin = pl.pallas_call(
        _linattn_prompt_kernel,
        grid=(batch, nchunk),
        in_specs=[
            pl.BlockSpec((CHUNK, W_LIN), lambda b, j: (b * nchunk + j, 0)),
            pl.BlockSpec((LANE, QK_L), lambda b, j: (0, 0)),
            pl.BlockSpec((1, QK_L), lambda b, j: (0, 0)),
            pl.BlockSpec((1, QK_L), lambda b, j: (0, 0)),
            pl.BlockSpec((1, V_L), lambda b, j: (0, 0)),
        ],
        out_specs=[
            pl.BlockSpec((CHUNK, 2 * V_L), lambda b, j: (b * nchunk + j, 0)),
            pl.BlockSpec((None, 2, QK_L, V_L), lambda b, j: (b, 0, 0, 0)),
        ],
        out_shape=[jax.ShapeDtypeStruct((batch * seq, 2 * V_L), F32),
                   jax.ShapeDtypeStruct((batch, 2, QK_L, V_L), F32)],
        scratch_shapes=[pltpu.VMEM((2, QK_L, V_L), F32)],
        compiler_params=_params("parallel", "arbitrary"),
        name="linattn_prompt",
    )(zl, wg, bg, lgam, ggla)
    sfin = sfin.reshape(batch, 2, H_L, DK_L, H_L, DV_L)
    return obc, jnp.stack([sfin[:, :, h, :, h, :] for h in range(H_L)], axis=2)


def _linattn_sample_kernel(q_ref, k_ref, gb_ref, v_ref, gate_ref, sg_ref, sr_ref, wgt_ref, bg_ref, lgam_ref,
                           ggla_ref, o_ref, s1_ref, *, n_seq, t_new):
    mix = pl.program_id(0)
    pre = _nn(wgt_ref[...], gb_ref[...], precision=HIGHEST) + bg_ref[...]
    la_gla = jax.nn.log_sigmoid(pre) / GATE_TAU
    la = jnp.where(mix == 0, la_gla, jnp.broadcast_to(lgam_ref[...], la_gla.shape))
    a = jnp.exp(la)
    q = q_ref[...] * (DK_L ** -0.5)
    k = k_ref[...]
    v = v_ref[...]
    s0 = jnp.where(mix == 0, sg_ref[...], sr_ref[...])
    s = [s0[d * DV_L:(d + 1) * DV_L, :] for d in range(DK_L)]
    outs = []
    for t in range(t_new):
        tok = slice(t * n_seq, (t + 1) * n_seq)
        vt = v[:, tok]
        ot = jnp.zeros((DV_L, n_seq), F32)
        for d in range(DK_L):
            s[d] = a[d:d + 1, tok] * s[d] + k[d:d + 1, tok] * vt
            ot = ot + q[d:d + 1, tok] * s[d]
        outs.append(ot)
    s1_ref[...] = jnp.concatenate(s, axis=0)
    o = jnp.concatenate(outs, axis=1)
    mean_sq = jnp.mean(o * o, axis=0, keepdims=True)
    o_gla = o * lax.rsqrt(mean_sq + NORM_EPS) * ggla_ref[...]
    dev = o - jnp.mean(o, axis=0, keepdims=True)
    o_ret = dev * lax.rsqrt(jnp.mean(dev * dev, axis=0, keepdims=True) + NORM_EPS)
    o_ref[...] = jnp.where(mix == 0, o_gla, o_ret) * jax.nn.silu(gate_ref[...])


def _linattn_sample(zlt, sg, sr, layer, wgt, bg_col, lgam_col, ggla_col, n_seq, t_new):
    ntok = t_new * n_seq
    state = pl.BlockSpec((None, None, DK_L * DV_L, n_seq), lambda m, h: (layer, h, 0, 0))
    q0, k0, g0, v0, r0 = 0, 2 * QK_L, 4 * QK_L, 4 * QK_L + LANE, 4 * QK_L + LANE + 2 * V_L
    o, s1 = pl.pallas_call(
        functools.partial(_linattn_sample_kernel, n_seq=n_seq, t_new=t_new),
        grid=(2, H_L),
        in_specs=[
            pl.BlockSpec((DK_L, ntok), lambda m, h: (q0 // DK_L + m * H_L + h, 0)),
            pl.BlockSpec((DK_L, ntok), lambda m, h: (k0 // DK_L + m * H_L + h, 0)),
            pl.BlockSpec((LANE, ntok), lambda m, h: (g0 // LANE, 0)),
            pl.BlockSpec((DV_L, ntok), lambda m, h: (v0 // DV_L + m * H_L + h, 0)),
            pl.BlockSpec((DV_L, ntok), lambda m, h: (r0 // DV_L + m * H_L + h, 0)),
            state, state,
            pl.BlockSpec((DK_L, LANE), lambda m, h: (h, 0)),
            pl.BlockSpec((DK_L, 1), lambda m, h: (h, 0)),
            pl.BlockSpec((DK_L, 1), lambda m, h: (h, 0)),
            pl.BlockSpec((DV_L, 1), lambda m, h: (0, 0)),
        ],
        out_specs=[
            pl.BlockSpec((DV_L, ntok), lambda m, h: (m * H_L + h, 0)),
            pl.BlockSpec((None, None, DK_L * DV_L, n_seq), lambda m, h: (m, h, 0, 0)),
        ],
        out_shape=[jax.ShapeDtypeStruct((2 * V_L, ntok), F32),
                   jax.ShapeDtypeStruct((2, H_L, DK_L * DV_L, n_seq), F32)],
        compiler_params=_params("parallel", "parallel"),
        name="linattn_sample",
    )(zlt, zlt, zlt, zlt, zlt, sg, sr, wgt, bg_col, lgam_col, ggla_col)
    return o, s1


def _out_proj_prompt_kernel(o1_ref, o2_ref, o3_ref, l1_ref, l2_ref, l3_ref, obc_ref, x_ref, w_ref, y_ref):
    l1, l2, l3 = l1_ref[...], l2_ref[...], l3_ref[...]
    m = jnp.maximum(jnp.maximum(l1, l2), l3)
    e1, e2, e3 = jnp.exp(l1 - m), jnp.exp(l2 - m), jnp.exp(l3 - m)
    oa = (e1 * o1_ref[...] + e2 * o2_ref[...] + e3 * o3_ref[...]) / (e1 + e2 + e3)
    y = _nn(oa.astype(BF16), w_ref[0:W_A, :]) + _nn(obc_ref[...].astype(BF16), w_ref[W_A:, :])
    y_ref[...] = x_ref[...] + y


def _out_proj_prompt(branches, obc, x, w, tm=512):
    t = x.shape[0]
    half = pl.BlockSpec((tm, W_A), lambda i: (i, 0))
    full = pl.BlockSpec((tm, D_MODEL), lambda i: (i, 0))
    (o1, l1), (o2, l2), (o3, l3) = branches
    return pl.pallas_call(
        _out_proj_prompt_kernel,
        grid=(t // tm,),
        in_specs=[half] * 7 + [full, pl.BlockSpec((D_MODEL, D_MODEL), lambda i: (0, 0))],
        out_specs=full,
        out_shape=jax.ShapeDtypeStruct((t, D_MODEL), F32),
        compiler_params=_params("parallel"),
        name="out_proj_prompt",
    )(o1, o2, o3, l1, l2, l3, obc, x, w)


def _out_proj_sample_kernel(oa_ref, obct_ref, x_ref, w_ref, y_ref):
    y = _nn(oa_ref[...].astype(BF16), w_ref[0:W_A, :]) + _tn(obct_ref[...].astype(BF16), w_ref[W_A:, :])
    y_ref[...] = x_ref[...] + y


def _out_proj_sample(oa, obct, x, w):
    return pl.pallas_call(
        _out_proj_sample_kernel,
        out_shape=jax.ShapeDtypeStruct(x.shape, F32),
        compiler_params=pltpu.CompilerParams(vmem_limit_bytes=VMEM_LIMIT),
        name="out_proj_sample",
    )(oa, obct, x, w)


def _bitonic_sort_desc(x):
    x = list(x)
    n = len(x)
    k = 2
    while k <= n:
        j = k // 2
        while j >= 1:
            for i in range(n):
                l = i ^ j
                if l > i:
                    hi, lo = jnp.maximum(x[i], x[l]), jnp.minimum(x[i], x[l])
                    x[i], x[l] = (hi, lo) if (i & k) == 0 else (lo, hi)
            j //= 2
        k *= 2
    return x


def _merge_top(a, b):
    n = len(a)
    x = [jnp.maximum(a[i], b[n - 1 - i]) for i in range(n)]
    j = n // 2
    while j >= 1:
        for i in range(n):
            l = i ^ j
            if l > i:
                x[i], x[l] = jnp.maximum(x[i], x[l]), jnp.minimum(x[i], x[l])
        j //= 2
    return x


def _top16_rows(s):
    t = s.shape[1]
    slabs = s.reshape(N_KEYS // SUBLANE, SUBLANE, t)
    x = _bitonic_sort_desc([slabs[i] for i in range(N_KEYS // SUBLANE)])
    for shift in (4, 2, 1):
        x = _merge_top(x, [pltpu.roll(xi, shift, 0) for xi in x])
    return x


def _top16_pair_sums(t1, t2):
    k = PEER_TOPK
    cand = [[t1[a] + t2[b] for b in range(k // (a + 1))] for a in range(k)]
    neg = jnp.full_like(t1[0], -jnp.inf)
    g0 = cand[0]
    g1 = _bitonic_sort_desc(cand[1] + cand[2] + cand[3][0:3])
    g2 = _bitonic_sort_desc(cand[3][3:4] + cand[4] + cand[5] + cand[6] + cand[7] + [cand[a][0] for a in range(8, 14)])
    g3 = [jnp.maximum(cand[14][0], cand[15][0]), jnp.minimum(cand[14][0], cand[15][0])] + [neg] * (k - 2)
    return _merge_top(_merge_top(g0, g1), _merge_top(g2, g3))


PEER_GROUP = 4
PEER_E_BLK = 2048


def _peer_kernel(x_ref, g_ref, wpqt_ref, keys_ref, u_ref, vt_ref, gfin_ref, y_ref,
                 hn_scr, rank_scr, f2_scr, nsel_scr, f1_scr, acc_scr, pre_scr, act_scr, *, final_norm):
    e = pl.program_id(1)
    tt = hn_scr.shape[0]
    e_blk = u_ref.shape[0]
    nslab = N_KEYS // SUBLANE
    grp_rows = PEER_GROUP * N_KEYS

    @pl.when(e == 0)
    def _route():
        hn_scr[...] = _rms(x_ref[...], g_ref[...]).astype(BF16)
        acc_scr[...] = jnp.zeros_like(acc_scr)

        def head(h, carry):
            w_rows = pl.ds(pl.multiple_of(h * 2 * N_KEYS, 2 * N_KEYS), 2 * N_KEYS)
            qk = _nt(wpqt_ref[w_rows, :], hn_scr[...])
            s1 = _nn(keys_ref[0], qk[0:N_KEYS], precision=HIGHEST)
            s2 = _nn(keys_ref[1], qk[N_KEYS:2 * N_KEYS], precision=HIGHEST)
            t1 = _top16_rows(s1)
            t2 = _top16_rows(s2)
            top = _top16_pair_sums(t1, t2)
            tau, best = top[PEER_TOPK - 1], top[0]
            z = jnp.exp(top[0] - best)
            for r in range(1, PEER_TOPK):
                z = z + jnp.exp(top[r] - best)
            s1s = s1.reshape(nslab, SUBLANE, tt)
            s2s = s2.reshape(nslab, SUBLANE, tt)
            nsel = jnp.zeros((nslab, SUBLANE, tt), F32)
            rank = jnp.zeros((nslab, SUBLANE, tt), F32)
            for r in range(PEER_TOPK):
                nsel = nsel + jnp.where(s1s + t2[r][None] >= tau[None], 1.0, 0.0)
                rank = rank + jnp.where(t2[r][None] > s2s, 1.0, 0.0)
            nsel_scr[h] = nsel.reshape(N_KEYS, tt)
            rank_scr[h] = rank.reshape(N_KEYS, tt).astype(BF16)
            f1_scr[h] = jnp.exp(s1s - t1[0][None]).reshape(N_KEYS, tt)
            f2_scr[h] = (jnp.exp(s2s - t2[0][None]) / z[None]).reshape(N_KEYS, tt).astype(BF16)
            return carry

        lax.fori_loop(0, PEER_HEADS, head, 0)

    def pre_activations(g):
        rows = slice(g * grp_rows, (g + 1) * grp_rows)
        pre_scr[g % 2] = _nt(u_ref[rows, :], hn_scr[...])

    def gated_activations(g):
        for al in range(PEER_GROUP):
            a = e * (e_blk // N_KEYS) + g * PEER_GROUP + al
            rows = slice(al * N_KEYS, (al + 1) * N_KEYS)
            n_rows = [nsel_scr[h, pl.ds(a, 1), :] for h in range(PEER_HEADS)]
            f1_rows = [f1_scr[h, pl.ds(a, 1), :] for h in range(PEER_HEADS)]
            for j in range(tt // LANE):
                ln = slice(j * LANE, (j + 1) * LANE)
                gate = jnp.zeros((N_KEYS, LANE), BF16)
                for h in range(PEER_HEADS):
                    n_row = jnp.broadcast_to(n_rows[h][:, ln], (N_KEYS, LANE)).astype(BF16)
                    f1_row = jnp.broadcast_to(f1_rows[h][:, ln], (N_KEYS, LANE)).astype(BF16)
                    gate = gate + jnp.where(rank_scr[h, :, ln] < n_row, f2_scr[h, :, ln], 0.0) * f1_row
                p = pre_scr[g % 2, rows, ln]
                gelu = 0.5 * p * (1.0 + lax.erf(p * (2.0 ** -0.5)))
                act_scr[g % 2, rows, ln] = gelu.astype(BF16) * gate

    def accumulate(g):
        cols = slice(g * grp_rows, (g + 1) * grp_rows)
        acc_scr[...] += _nn(vt_ref[:, cols], act_scr[g % 2])

    n_grp = e_blk // grp_rows
    pre_activations(0)
    for g in range(n_grp):
        if g + 1 < n_grp:
            pre_activations(g + 1)
        gated_activations(g)
        if g >= 1:
            accumulate(g - 1)
    accumulate(n_grp - 1)

    @pl.when(e == pl.num_programs(1) - 1)
    def _finish():
        y = x_ref[...] + acc_scr[...].T
        if final_norm:
            y = _rms(y, gfin_ref[...])
        y_ref[...] = y


def _peer(x, g, wpqt, keys, u, vtb, gfin, final_norm, tt):
    t = x.shape[0]
    e_blk = PEER_E_BLK
    once = dict(pipeline_mode=pl.Buffered(1))
    return pl.pallas_call(
        functools.partial(_peer_kernel, final_norm=final_norm),
        grid=(t // tt, N_EXPERTS // e_blk),
        in_specs=[
            pl.BlockSpec((tt, D_MODEL), lambda i, e: (i, 0), **once),
            pl.BlockSpec((1, D_MODEL), lambda i, e: (0, 0)),
            pl.BlockSpec((2 * PEER_HEADS * N_KEYS, D_MODEL), lambda i, e: (0, 0), **once),
            pl.BlockSpec((2, N_KEYS, N_KEYS), lambda i, e: (0, 0, 0)),
            pl.BlockSpec((e_blk, D_MODEL), lambda i, e: (e, 0)),
            pl.BlockSpec((None, D_MODEL, e_blk), lambda i, e: (e, 0, 0)),
            pl.BlockSpec((1, D_MODEL), lambda i, e: (0, 0)),
        ],
        out_specs=pl.BlockSpec((tt, D_MODEL), lambda i, e: (i, 0), **once),
        out_shape=jax.ShapeDtypeStruct((t, D_MODEL), F32),
        scratch_shapes=[
            pltpu.VMEM((tt, D_MODEL), BF16),
            pltpu.VMEM((PEER_HEADS, N_KEYS, tt), BF16),
            pltpu.VMEM((PEER_HEADS, N_KEYS, tt), BF16),
            pltpu.VMEM((PEER_HEADS, N_KEYS, tt), F32),
            pltpu.VMEM((PEER_HEADS, N_KEYS, tt), F32),
            pltpu.VMEM((D_MODEL, tt), F32),
            pltpu.VMEM((2, PEER_GROUP * N_KEYS, tt), F32),
            pltpu.VMEM((2, PEER_GROUP * N_KEYS, tt), BF16),
        ],
        compiler_params=_params("parallel", "arbitrary", vmem=VMEM_PHYSICAL * 15 // 16),
        name="peer",
    )(x, g, wpqt, keys, u, vtb, gfin)


def _split_w_in(w):
    sizes = (W_A, W_A, W_A, QK_L, QK_L, V_L, V_L, GATE_RANK, QK_L, QK_L, V_L, V_L)
    out, start = [], 0
    for n in sizes:
        out.append(w[:, start:start + n])
        start += n
    return out


def _layer_weights(w_in, w_gate2, b_gate, g_gla, w_out, w_pq, u_tab, v_tab):
    qa, ka, va, qb, kb, vb, rb, gb, qc, kc, vc, gc = _split_w_in(w_in)
    gb = jnp.pad(gb, ((0, 0), (0, LANE - GATE_RANK)))
    wa = jnp.concatenate([qa, ka, va], axis=1).astype(BF16)
    wl = jnp.concatenate([qb, qc, kb, kc, gb, vb, vc, rb, gc], axis=1).astype(BF16)
    wg = jnp.pad(w_gate2, ((0, LANE - GATE_RANK), (0, 0)))
    log_gamma = jnp.log(1.0 - 2.0 ** (-5.0 - jnp.arange(H_L, dtype=F32)))
    lgam = jnp.repeat(log_gamma, DK_L)
    return dict(
        wa=wa, wl=wl, wlt=wl.T, wg=wg, wgt=wg.T,
        bg=b_gate.reshape(1, QK_L), bg_col=b_gate.reshape(QK_L, 1),
        lgam=lgam.reshape(1, QK_L), lgam_col=lgam.reshape(QK_L, 1),
        ggla=jnp.tile(g_gla, H_L).reshape(1, V_L), ggla_col=g_gla.reshape(DV_L, 1),
        w_out=w_out.astype(BF16), wpqt=w_pq.T.astype(BF16),
        u=u_tab.astype(BF16),
        vtb=v_tab.reshape(N_EXPERTS // PEER_E_BLK, PEER_E_BLK, D_MODEL).transpose(0, 2, 1).astype(BF16),
    )


def kernel(x_prompt, x_sample, cache_kv_win, state_gla, state_ret, w_in, w_gate2, b_gate, g_gla,
           w_out, g_mix, g_ffn, w_pq, sub_keys, u_tab, v_tab, g_final):
    batch, seq, _ = x_prompt.shape
    n_seq, t_new, _ = x_sample.shape
    depth = w_in.shape[0]
    win = min(MAX_WINDOW, seq)
    xp = x_prompt.reshape(batch * seq, D_MODEL)
    xs = x_sample.transpose(1, 0, 2).reshape(t_new * n_seq, D_MODEL)
    tables = _sample_attn_tables(t_new)
    gfin = g_final.reshape(1, D_MODEL)
    cache_t = cache_kv_win.transpose(0, 1, 3, 4, 5, 2)
    sg = state_gla.transpose(0, 2, 3, 4, 1).reshape(depth, H_L, DK_L * DV_L, n_seq)
    sr = state_ret.transpose(0, 2, 3, 4, 1).reshape(depth, H_L, DK_L * DV_L, n_seq)
    kv_p, kv_s, gla_p, gla_s, ret_p, ret_s = [], [], [], [], [], []
    for l in range(depth):
        w = _layer_weights(w_in[l], w_gate2[l], b_gate[l], g_gla[l], w_out[l], w_pq[l], u_tab[l], v_tab[l])
        gm = g_mix[l].reshape(1, D_MODEL)
        gf = g_ffn[l].reshape(1, D_MODEL)
        last = l == depth - 1

        za, zl = _in_proj(xp, gm, w["wa"], w["wl"])
        branches = [_attn_prompt_branch(za, batch, seq, d) for (_, d) in BRANCHES]
        obc, sfin = _linattn_prompt(zl, w["wg"], w["bg"], w["lgam"], w["ggla"], batch, seq)
        xp = _out_proj_prompt(branches, obc, xp, w["w_out"])
        xp = _peer(xp, gf, w["wpqt"], sub_keys[l], w["u"], w["vtb"], gfin, last, tt=1024)
        kv = za.reshape(batch, seq, 3, H_A, HEAD_DIM)[:, seq - win:, 1:3]
        kv_p.append(kv)
        gla_p.append(sfin[:, 0])
        ret_p.append(sfin[:, 1])

        za_s, zlt = _in_proj_sample(xs, gm, w["wa"], w["wlt"])
        oa_s = _attn_sample(za_s, cache_t, l, tables, n_seq, t_new)
        obct, s1 = _linattn_sample(zlt, sg, sr, l, w["wgt"], w["bg_col"], w["lgam_col"], w["ggla_col"],
                                   n_seq, t_new)
        xs = _out_proj_sample(oa_s, obct, xs, w["w_out"])
        xs = _peer(xs, gf, w["wpqt"], sub_keys[l], w["u"], w["vtb"], gfin, last, tt=512)
        kv_new = za_s.reshape(t_new, n_seq, 3, H_A, HEAD_DIM)[:, :, 1:3].transpose(1, 0, 2, 3, 4)
        kv_s.append(kv_new)
        gla_s.append(s1[0])
        ret_s.append(s1[1])

    y_prompt = xp.reshape(batch, seq, D_MODEL)
    y_sample = xs.reshape(t_new, n_seq, D_MODEL).transpose(1, 0, 2)
    seq_major = lambda s: jnp.stack(s).reshape(depth, H_L, DK_L, DV_L, n_seq).transpose(0, 4, 1, 2, 3)
    return (y_prompt, y_sample, jnp.stack(kv_p), jnp.stack(kv_s), jnp.stack(gla_p), seq_major(gla_s),
            jnp.stack(ret_p), seq_major(ret_s))
```

```python
import functools
import math

import numpy as np
import jax
import jax.numpy as jnp
from jax import lax
from jax.experimental import pallas as pl
from jax.experimental.pallas import tpu as pltpu

F32 = jnp.float32
BF16 = jnp.bfloat16
HIGHEST = lax.Precision.HIGHEST

D_MODEL = 1024
HEAD_DIM = 64
H_A = 8
W_A = H_A * HEAD_DIM
BRANCHES = ((128, 1), (512, 4), (2048, 16))
BAND = 128
MAX_WINDOW = 2048
H_L = 4
DK_L = 32
DV_L = 64
QK_L = H_L * DK_L
V_L = H_L * DV_L
GATE_RANK = 16
GATE_TAU = 16.0
N_KEYS = 128
N_EXPERTS = N_KEYS * N_KEYS
PEER_HEADS = 8
PEER_TOPK = 16
NORM_EPS = 1e-6
W_LIN = 2 * 2 * QK_L + 128 + 2 * V_L + 2 * V_L

LANE = 128
SUBLANE = 8
VMEM_PHYSICAL = 64 * 1024 * 1024
VMEM_LIMIT = VMEM_PHYSICAL * 7 // 8


def _params(*sem, vmem=VMEM_LIMIT):
    return pltpu.CompilerParams(dimension_semantics=sem, vmem_limit_bytes=vmem)


def _rms(x, g):
    return x * lax.rsqrt(jnp.mean(x * x, axis=-1, keepdims=True) + NORM_EPS) * g


def _nt(a, b, **kw):
    return lax.dot_general(a, b, (((1,), (1,)), ((), ())), preferred_element_type=F32, **kw)


def _tn(a, b, **kw):
    return lax.dot_general(a, b, (((0,), (0,)), ((), ())), preferred_element_type=F32, **kw)


def _nn(a, b, **kw):
    return jnp.dot(a, b, preferred_element_type=F32, **kw)


def _in_proj_kernel(x_ref, g_ref, wa_ref, wl_ref, za_ref, zl_ref):
    h = _rms(x_ref[...], g_ref[...]).astype(BF16)
    za_ref[...] = _nn(h, wa_ref[...])
    zl_ref[...] = _nn(h, wl_ref[...])


def _in_proj(x, g, wa, wl, tm=256):
    t = x.shape[0]
    return pl.pallas_call(
        _in_proj_kernel,
        grid=(t // tm,),
        in_specs=[
            pl.BlockSpec((tm, D_MODEL), lambda i: (i, 0)),
            pl.BlockSpec((1, D_MODEL), lambda i: (0, 0)),
            pl.BlockSpec((D_MODEL, 3 * W_A), lambda i: (0, 0)),
            pl.BlockSpec((D_MODEL, W_LIN), lambda i: (0, 0)),
        ],
        out_specs=[
            pl.BlockSpec((tm, 3 * W_A), lambda i: (i, 0)),
            pl.BlockSpec((tm, W_LIN), lambda i: (i, 0)),
        ],
        out_shape=[jax.ShapeDtypeStruct((t, 3 * W_A), F32), jax.ShapeDtypeStruct((t, W_LIN), F32)],
        compiler_params=_params("parallel"),
        name="in_proj",
    )(x, g, wa, wl)


def _in_proj_sample_kernel(x_ref, g_ref, wa_ref, wlt_ref, za_ref, zlt_ref):
    h = _rms(x_ref[...], g_ref[...]).astype(BF16)
    za_ref[...] = _nn(h, wa_ref[...])
    zlt_ref[...] = _nt(wlt_ref[...], h)


def _in_proj_sample(x, g, wa, wlt):
    t = x.shape[0]
    return pl.pallas_call(
        _in_proj_sample_kernel,
        out_shape=[jax.ShapeDtypeStruct((t, 3 * W_A), F32), jax.ShapeDtypeStruct((W_LIN, t), F32)],
        compiler_params=pltpu.CompilerParams(vmem_limit_bytes=VMEM_LIMIT),
        name="in_proj_sample",
    )(x, g, wa, wlt)


def _alibi_slope(h):
    return 2.0 ** (-8.0 * (h + 1) / H_A)


ATT_TILE = BAND * max(d for _, d in BRANCHES)


def _attn_prompt_kernel(slope_ref, q_ref, kc_ref, kp_ref, vc_ref, vp_ref, o_ref,
                        k_scr, v_scr, m_scr, l_scr, acc_scr):
    tile = pl.program_id(2)
    k_scr[0:ATT_TILE] = kp_ref[...]
    k_scr[ATT_TILE:2 * ATT_TILE] = kc_ref[...]
    v_scr[0:ATT_TILE] = vp_ref[...]
    v_scr[ATT_TILE:2 * ATT_TILE] = vc_ref[...]
    qi = lax.broadcasted_iota(jnp.int32, (BAND, 2 * BAND), 0)
    ki = lax.broadcasted_iota(jnp.int32, (BAND, 2 * BAND), 1)
    dist = qi + BAND - ki
    in_band = (dist >= 0) & (dist <= BAND)
    distf = dist.astype(F32)
    low = lax.broadcasted_iota(jnp.int32, (BAND, LANE), 1) < HEAD_DIM
    slopes = [slope_ref[sub][0:1, 0:1] for sub in range(2)]

    def unit(u, carry, *, d, first):
        res, blk = u % d, u // d
        start = res + d * BAND * blk
        stride = None if d == 1 else d
        rows = pl.ds(start, BAND, stride=stride)
        keys = pl.ds(ATT_TILE + start - d * BAND, 2 * BAND, stride=stride)
        q = q_ref[rows, :]
        k = k_scr[keys, :].astype(BF16)
        v = v_scr[keys, :].astype(BF16)
        first_key = jnp.where((tile == 0) & (blk == 0), BAND, 0)
        valid = in_band & (ki >= first_key)
        ms, ls, pvs = [], [], []
        for sub in range(2):
            qm = jnp.where(low if sub == 0 else jnp.logical_not(low), q, 0.0).astype(BF16)
            s = _nt(qm, k) * (HEAD_DIM ** -0.5)
            s = jnp.where(valid, s - (slopes[sub] * float(d)) * distf, -jnp.inf)
            m = jnp.max(s, axis=-1, keepdims=True)
            e = jnp.exp(s - m)
            ms.append(m)
            ls.append(jnp.sum(e, axis=-1, keepdims=True))
            pvs.append(_nn(e.astype(BF16), v))
        m_u = jnp.where(low, ms[0], ms[1])
        l_u = jnp.where(low, ls[0], ls[1])
        acc_u = jnp.where(low, pvs[0], pvs[1])
        if first:
            m_scr[rows, :] = m_u
            l_scr[rows, :] = l_u
            acc_scr[rows, :] = acc_u
        else:
            m_o = m_scr[rows, :]
            m_n = jnp.maximum(m_o, m_u)
            w_o, w_u = jnp.exp(m_o - m_n), jnp.exp(m_u - m_n)
            m_scr[rows, :] = m_n
            l_scr[rows, :] = w_o * l_scr[rows, :] + w_u * l_u
            acc_scr[rows, :] = w_o * acc_scr[rows, :] + w_u * acc_u
        return carry

    for n, (_, d) in enumerate(BRANCHES):
        lax.fori_loop(0, ATT_TILE // BAND, functools.partial(unit, d=d, first=n == 0), 0)
    o_ref[...] = acc_scr[...] / l_scr[...]


def _attn_prompt(za, batch, seq):
    ntile = seq // ATT_TILE
    npair = H_A // 2
    slope_tab = jnp.asarray(np.broadcast_to(
        np.array([_alibi_slope(h) for h in range(H_A)], np.float32).reshape(npair, 2, 1, 1),
        (npair, 2, SUBLANE, LANE)))
    blk = (ATT_TILE, LANE)
    prev = lambda t: jnp.maximum(t - 1, 0)
    return pl.pallas_call(
        _attn_prompt_kernel,
        grid=(batch, npair, ntile),
        in_specs=[
            pl.BlockSpec((None, 2, SUBLANE, LANE), lambda b, p, t: (p, 0, 0, 0)),
            pl.BlockSpec(blk, lambda b, p, t: (b * ntile + t, p)),
            pl.BlockSpec(blk, lambda b, p, t: (b * ntile + t, npair + p)),
            pl.BlockSpec(blk, lambda b, p, t: (b * ntile + prev(t), npair + p)),
            pl.BlockSpec(blk, lambda b, p, t: (b * ntile + t, 2 * npair + p)),
            pl.BlockSpec(blk, lambda b, p, t: (b * ntile + prev(t), 2 * npair + p)),
        ],
        out_specs=pl.BlockSpec(blk, lambda b, p, t: (b * ntile + t, p)),
        out_shape=jax.ShapeDtypeStruct((batch * seq, W_A), F32),
        scratch_shapes=[pltpu.VMEM((2 * ATT_TILE, LANE), F32), pltpu.VMEM((2 * ATT_TILE, LANE), F32),
                        pltpu.VMEM((ATT_TILE, LANE), F32), pltpu.VMEM((ATT_TILE, LANE), F32),
                        pltpu.VMEM((ATT_TILE, LANE), F32)],
        compiler_params=_params("parallel", "parallel", "arbitrary"),
        name="attn_prompt",
    )(slope_tab, za, za, za, za, za)


def _sample_attn_tables(t_new):
    n_past = MAX_WINDOW

    def entry(h, t, idx):
        delta = n_past + t - idx
        if delta < 0:
            return -np.inf
        cnt = sum(1 for (w, d) in BRANCHES if delta % d == 0 and delta <= w)
        return -_alibi_slope(h) * delta + math.log(cnt) if cnt else -np.inf

    past = np.zeros((H_A, SUBLANE, MAX_WINDOW), np.float32)
    new = np.zeros((H_A, SUBLANE, SUBLANE), np.float32)
    for h in range(H_A):
        for t in range(t_new):
            past[h, t] = [entry(h, t, idx) for idx in range(MAX_WINDOW)]
            new[h, t] = [entry(h, t, n_past + s) if s < t_new else -np.inf for s in range(SUBLANE)]
    return jnp.asarray(past), jnp.asarray(new)


def _attn_sample_kernel(za_ref, kt_ref, vt_ref, bp_ref, bn_ref, o_ref, *, t_new):
    zero_rows = jnp.zeros((SUBLANE - t_new, W_A), F32)
    rows = [za_ref[t] for t in range(t_new)]
    q8 = jnp.concatenate([r[:, 0:W_A] for r in rows] + [zero_rows], axis=0)
    k8 = jnp.concatenate([r[:, W_A:2 * W_A] for r in rows] + [zero_rows], axis=0)
    v8 = jnp.concatenate([r[:, 2 * W_A:3 * W_A] for r in rows] + [zero_rows], axis=0)
    scale = HEAD_DIM ** -0.5
    outs = []
    for h in range(H_A):
        sl = slice(h * HEAD_DIM, (h + 1) * HEAD_DIM)
        qh = q8[:, sl].astype(BF16)
        s_past = _nn(qh, kt_ref[h].astype(BF16)) * scale + bp_ref[h]
        s_new = _nt(qh, k8[:, sl].astype(BF16)) * scale + bn_ref[h]
        m = jnp.maximum(s_past.max(axis=-1, keepdims=True), s_new.max(axis=-1, keepdims=True))
        p_past = jnp.exp(s_past - m)
        p_new = jnp.exp(s_new - m)
        den = p_past.sum(axis=-1, keepdims=True) + p_new.sum(axis=-1, keepdims=True)
        o = _nt(p_past.astype(BF16), vt_ref[h].astype(BF16)) + _nn(p_new.astype(BF16), v8[:, sl].astype(BF16))
        outs.append(o / den)
    out = jnp.concatenate(outs, axis=-1)
    for t in range(t_new):
        o_ref[t] = out[t:t + 1, :]


def _attn_sample(za, cache_t, layer, tables, n_seq, t_new):
    bp, bn = tables
    zav = za.reshape(t_new, n_seq, 1, 3 * W_A)
    window = lambda kv: pl.BlockSpec((None, None, None, H_A, HEAD_DIM, MAX_WINDOW),
                                     lambda b: (layer, b, kv, 0, 0, 0))
    o = pl.pallas_call(
        functools.partial(_attn_sample_kernel, t_new=t_new),
        grid=(n_seq,),
        in_specs=[
            pl.BlockSpec((t_new, None, 1, 3 * W_A), lambda b: (0, b, 0, 0)),
            window(0), window(1),
            pl.BlockSpec(bp.shape, lambda b: (0, 0, 0)),
            pl.BlockSpec(bn.shape, lambda b: (0, 0, 0)),
        ],
        out_specs=pl.BlockSpec((t_new, None, 1, W_A), lambda b: (0, b, 0, 0)),
        out_shape=jax.ShapeDtypeStruct((t_new, n_seq, 1, W_A), F32),
        compiler_params=_params("parallel"),
        name="attn_sample",
    )(zav, cache_t, cache_t, bp, bn)
    return o.reshape(t_new * n_seq, W_A)


CHUNK = 128


def _linattn_prompt_kernel(zl_ref, wg_ref, bg_ref, lgam_ref, ggla_ref, obc_ref, sfin_ref, s_scr):
    j = pl.program_id(1)

    @pl.when(j == 0)
    def _():
        s_scr[...] = jnp.zeros_like(s_scr)

    c = CHUNK
    z = zl_ref[...]
    q = z[:, 0:2 * QK_L] * (DK_L ** -0.5)
    k = z[:, 2 * QK_L:4 * QK_L]
    gb = z[:, 4 * QK_L:4 * QK_L + LANE]
    v = z[:, 4 * QK_L + LANE:4 * QK_L + LANE + 2 * V_L]
    gates = z[:, 4 * QK_L + LANE + 2 * V_L:]

    pre = _nn(gb, wg_ref[...], precision=HIGHEST) + bg_ref[...]
    la = jnp.concatenate([jax.nn.log_sigmoid(pre) / GATE_TAU, jnp.broadcast_to(lgam_ref[...], (c, QK_L))], axis=1)
    row = lax.broadcasted_iota(jnp.int32, (c, c), 0)
    col = lax.broadcasted_iota(jnp.int32, (c, c), 1)
    causal = col <= row
    b = _nn(causal.astype(F32), la, precision=HIGHEST)
    mid = b[c // 2 - 1:c // 2, :]
    last = b[c - 1:c, :]
    qt = q * jnp.exp(b - mid)
    kt = k * jnp.exp(mid - b)
    qi = (q * jnp.exp(b)).astype(BF16)
    kh = (k * jnp.exp(last - b)).astype(BF16)
    ones = jnp.ones((c, LANE), F32)

    head_qk = lax.broadcasted_iota(jnp.int32, (c, QK_L), 1) // DK_L
    head_v = lax.broadcasted_iota(jnp.int32, (c, V_L), 1) // DV_L
    blockdiag = (lax.broadcasted_iota(jnp.int32, (QK_L, V_L), 0) // DK_L
                 == lax.broadcasted_iota(jnp.int32, (QK_L, V_L), 1) // DV_L)
    causal4 = (lax.broadcasted_iota(jnp.int32, (H_L * c, c), 1)
               <= lax.broadcasted_iota(jnp.int32, (H_L * c, c), 0) % c)
    outs = []
    for mix in range(2):
        sl = slice(mix * QK_L, (mix + 1) * QK_L)
        vm = v[:, mix * V_L:(mix + 1) * V_L].astype(BF16)
        qstack = jnp.concatenate([jnp.where(head_qk == h, qt[:, sl], 0.0) for h in range(H_L)], axis=0)
        att = _nt(qstack.astype(BF16), kt[:, sl].astype(BF16))
        att = jnp.where(causal4, att, 0.0).astype(BF16)
        r = _nn(att, vm)
        o = jnp.zeros((c, V_L), F32)
        for h in range(H_L):
            o = o + jnp.where(head_v == h, r[h * c:(h + 1) * c, :], 0.0)
        s = s_scr[mix]
        o = o + _nn(qi[:, sl], s.astype(BF16))
        decay = jnp.exp(_tn(la[:, sl], ones, precision=HIGHEST))
        kv = _tn(kh[:, sl], vm)
        s_scr[mix] = jnp.concatenate([decay, decay], axis=1) * s + jnp.where(blockdiag, kv, 0.0)
        outs.append(o)

    seg = (lax.broadcasted_iota(jnp.int32, (V_L, V_L), 0) // DV_L
           == lax.broadcasted_iota(jnp.int32, (V_L, V_L), 1) // DV_L).astype(F32) * (1.0 / DV_L)
    ob = outs[0]
    ob = ob * lax.rsqrt(_nn(ob * ob, seg, precision=HIGHEST) + NORM_EPS) * ggla_ref[...]
    ob = ob * jax.nn.silu(gates[:, 0:V_L])
    oc = outs[1]
    dev = oc - _nn(oc, seg, precision=HIGHEST)
    oc = dev * lax.rsqrt(_nn(dev * dev, seg, precision=HIGHEST) + NORM_EPS) * jax.nn.silu(gates[:, V_L:2 * V_L])
    obc_ref[...] = jnp.concatenate([ob, oc], axis=1)

    @pl.when(j == pl.num_programs(1) - 1)
    def _():
        sfin_ref[...] = s_scr[...]


def _linattn_prompt(zl, wg, bg, lgam, ggla, batch, seq):
    nchunk = seq // CHUNK
    obc, sfin = pl.pallas_call(
        _linattn_prompt_kernel,
        grid=(batch, nchunk),
        in_specs=[
            pl.BlockSpec((CHUNK, W_LIN), lambda b, j: (b * nchunk + j, 0)),
            pl.BlockSpec((LANE, QK_L), lambda b, j: (0, 0)),
            pl.BlockSpec((1, QK_L), lambda b, j: (0, 0)),
            pl.BlockSpec((1, QK_L), lambda b, j: (0, 0)),
            pl.BlockSpec((1, V_L), lambda b, j: (0, 0)),
        ],
        out_specs=[
            pl.BlockSpec((CHUNK, 2 * V_L), lambda b, j: (b * nchunk + j, 0)),
            pl.BlockSpec((None, 2, QK_L, V_L), lambda b, j: (b, 0, 0, 0)),
        ],
        out_shape=[jax.ShapeDtypeStruct((batch * seq, 2 * V_L), F32),
                   jax.ShapeDtypeStruct((batch, 2, QK_L, V_L), F32)],
        scratch_shapes=[pltpu.VMEM((2, QK_L, V_L), F32)],
        compiler_params=_params("parallel", "arbitrary"),
        name="linattn_prompt",
    )(zl, wg, bg, lgam, ggla)
    sfin = sfin.reshape(batch, 2, H_L, DK_L, H_L, DV_L)
    return obc, jnp.stack([sfin[:, :, h, :, h, :] for h in range(H_L)], axis=2)


def _linattn_sample_kernel(q_ref, k_ref, gb_ref, v_ref, gate_ref, sg_ref, sr_ref, wgt_ref, bg_ref, lgam_ref,
                           ggla_ref, o_ref, s1_ref, *, n_seq, t_new):
    mix = pl.program_id(0)
    pre = _nn(wgt_ref[...], gb_ref[...], precision=HIGHEST) + bg_ref[...]
    la_gla = jax.nn.log_sigmoid(pre) / GATE_TAU
    la = jnp.where(mix == 0, la_gla, jnp.broadcast_to(lgam_ref[...], la_gla.shape))
    a = jnp.exp(la)
    q = q_ref[...] * (DK_L ** -0.5)
    k = k_ref[...]
    v = v_ref[...]
    s0 = jnp.where(mix == 0, sg_ref[...], sr_ref[...])
    s = [s0[d * DV_L:(d + 1) * DV_L, :] for d in range(DK_L)]
    outs = []
    for t in range(t_new):
        tok = slice(t * n_seq, (t + 1) * n_seq)
        vt = v[:, tok]
        ot = jnp.zeros((DV_L, n_seq), F32)
        for d in range(DK_L):
            s[d] = a[d:d + 1, tok] * s[d] + k[d:d + 1, tok] * vt
            ot = ot + q[d:d + 1, tok] * s[d]
        outs.append(ot)
    s1_ref[...] = jnp.concatenate(s, axis=0)
    o = jnp.concatenate(outs, axis=1)
    mean_sq = jnp.mean(o * o, axis=0, keepdims=True)
    o_gla = o * lax.rsqrt(mean_sq + NORM_EPS) * ggla_ref[...]
    dev = o - jnp.mean(o, axis=0, keepdims=True)
    o_ret = dev * lax.rsqrt(jnp.mean(dev * dev, axis=0, keepdims=True) + NORM_EPS)
    o_ref[...] = jnp.where(mix == 0, o_gla, o_ret) * jax.nn.silu(gate_ref[...])


def _linattn_sample(zlt, sg, sr, layer, wgt, bg_col, lgam_col, ggla_col, n_seq, t_new):
    ntok = t_new * n_seq
    state = pl.BlockSpec((None, None, DK_L * DV_L, n_seq), lambda m, h: (layer, h, 0, 0))
    q0, k0, g0, v0, r0 = 0, 2 * QK_L, 4 * QK_L, 4 * QK_L + LANE, 4 * QK_L + LANE + 2 * V_L
    o, s1 = pl.pallas_call(
        functools.partial(_linattn_sample_kernel, n_seq=n_seq, t_new=t_new),
        grid=(2, H_L),
        in_specs=[
            pl.BlockSpec((DK_L, ntok), lambda m, h: (q0 // DK_L + m * H_L + h, 0)),
            pl.BlockSpec((DK_L, ntok), lambda m, h: (k0 // DK_L + m * H_L + h, 0)),
            pl.BlockSpec((LANE, ntok), lambda m, h: (g0 // LANE, 0)),
            pl.BlockSpec((DV_L, ntok), lambda m, h: (v0 // DV_L + m * H_L + h, 0)),
            pl.BlockSpec((DV_L, ntok), lambda m, h: (r0 // DV_L + m * H_L + h, 0)),
            state, state,
            pl.BlockSpec((DK_L, LANE), lambda m, h: (h, 0)),
            pl.BlockSpec((DK_L, 1), lambda m, h: (h, 0)),
            pl.BlockSpec((DK_L, 1), lambda m, h: (h, 0)),
            pl.BlockSpec((DV_L, 1), lambda m, h: (0, 0)),
        ],
        out_specs=[
            pl.BlockSpec((DV_L, ntok), lambda m, h: (m * H_L + h, 0)),
            pl.BlockSpec((None, None, DK_L * DV_L, n_seq), lambda m, h: (m, h, 0, 0)),
        ],
        out_shape=[jax.ShapeDtypeStruct((2 * V_L, ntok), F32),
                   jax.ShapeDtypeStruct((2, H_L, DK_L * DV_L, n_seq), F32)],
        compiler_params=_params("parallel", "parallel"),
        name="linattn_sample",
    )(zlt, zlt, zlt, zlt, zlt, sg, sr, wgt, bg_col, lgam_col, ggla_col)
    return o, s1


def _out_proj_prompt_kernel(oa_ref, obc_ref, x_ref, w_ref, y_ref):
    y = _nn(oa_ref[...].astype(BF16), w_ref[0:W_A, :]) + _nn(obc_ref[...].astype(BF16), w_ref[W_A:, :])
    y_ref[...] = x_ref[...] + y


def _out_proj_prompt(oa, obc, x, w, tm=512):
    t = x.shape[0]
    half = pl.BlockSpec((tm, W_A), lambda i: (i, 0))
    full = pl.BlockSpec((tm, D_MODEL), lambda i: (i, 0))
    return pl.pallas_call(
        _out_proj_prompt_kernel,
        grid=(t // tm,),
        in_specs=[half, half, full, pl.BlockSpec((D_MODEL, D_MODEL), lambda i: (0, 0))],
        out_specs=full,
        out_shape=jax.ShapeDtypeStruct((t, D_MODEL), F32),
        compiler_params=_params("parallel"),
        name="out_proj_prompt",
    )(oa, obc, x, w)


def _out_proj_sample_kernel(oa_ref, obct_ref, x_ref, w_ref, y_ref):
    y = _nn(oa_ref[...].astype(BF16), w_ref[0:W_A, :]) + _tn(obct_ref[...].astype(BF16), w_ref[W_A:, :])
    y_ref[...] = x_ref[...] + y


def _out_proj_sample(oa, obct, x, w):
    return pl.pallas_call(
        _out_proj_sample_kernel,
        out_shape=jax.ShapeDtypeStruct(x.shape, F32),
        compiler_params=pltpu.CompilerParams(vmem_limit_bytes=VMEM_LIMIT),
        name="out_proj_sample",
    )(oa, obct, x, w)


def _bitonic_sort_desc(x):
    x = list(x)
    n = len(x)
    k = 2
    while k <= n:
        j = k // 2
        while j >= 1:
            for i in range(n):
                l = i ^ j
                if l > i:
                    hi, lo = jnp.maximum(x[i], x[l]), jnp.minimum(x[i], x[l])
                    x[i], x[l] = (hi, lo) if (i & k) == 0 else (lo, hi)
            j //= 2
        k *= 2
    return x


def _merge_top(a, b):
    n = len(a)
    x = [jnp.maximum(a[i], b[n - 1 - i]) for i in range(n)]
    j = n // 2
    while j >= 1:
        for i in range(n):
            l = i ^ j
            if l > i:
                x[i], x[l] = jnp.maximum(x[i], x[l]), jnp.minimum(x[i], x[l])
        j //= 2
    return x


def _top16_rows(s):
    t = s.shape[1]
    slabs = s.reshape(N_KEYS // SUBLANE, SUBLANE, t)
    x = _bitonic_sort_desc([slabs[i] for i in range(N_KEYS // SUBLANE)])
    for shift in (4, 2, 1):
        x = _merge_top(x, [pltpu.roll(xi, shift, 0) for xi in x])
    return x


def _top16_pair_sums(t1, t2):
    k = PEER_TOPK
    cand = [[t1[a] + t2[b] for b in range(k // (a + 1))] for a in range(k)]
    neg = jnp.full_like(t1[0], -jnp.inf)
    g0 = cand[0]
    g1 = _bitonic_sort_desc(cand[1] + cand[2] + cand[3][0:3])
    g2 = _bitonic_sort_desc(cand[3][3:4] + cand[4] + cand[5] + cand[6] + cand[7] + [cand[a][0] for a in range(8, 14)])
    g3 = [jnp.maximum(cand[14][0], cand[15][0]), jnp.minimum(cand[14][0], cand[15][0])] + [neg] * (k - 2)
    return _merge_top(_merge_top(g0, g1), _merge_top(g2, g3)), cand


PEER_GROUP = 4
PEER_E_BLK = 2048


def _peer_kernel(x_ref, g_ref, wpqt_ref, keys_ref, u_ref, vt_ref, gfin_ref, y_ref,
                 hn_scr, rank_scr, f2_scr, nsel_scr, f1_scr, acc_scr, pre_scr, act_scr, *, final_norm):
    e = pl.program_id(1)
    tt = hn_scr.shape[0]
    e_blk = u_ref.shape[0]
    nslab = N_KEYS // SUBLANE
    grp_rows = PEER_GROUP * N_KEYS
    nj = tt // LANE

    @pl.when(e == 0)
    def _route():
        hn_scr[...] = _rms(x_ref[...], g_ref[...]).astype(BF16)
        acc_scr[...] = jnp.zeros_like(acc_scr)

        def head(h, carry):
            w_rows = pl.ds(pl.multiple_of(h * 2 * N_KEYS, 2 * N_KEYS), 2 * N_KEYS)
            qk = _nt(wpqt_ref[w_rows, :], hn_scr[...])
            s1 = _nn(keys_ref[0], qk[0:N_KEYS], precision=HIGHEST)
            s2 = _nn(keys_ref[1], qk[N_KEYS:2 * N_KEYS], precision=HIGHEST)
            t1 = _top16_rows(s1)
            t2 = _top16_rows(s2)
            top, cand = _top16_pair_sums(t1, t2)
            tau, best = top[PEER_TOPK - 1], top[0]
            z = jnp.exp(top[0] - best)
            for r in range(1, PEER_TOPK):
                z = z + jnp.exp(top[r] - best)
            s1s = s1.reshape(nslab, SUBLANE, tt)
            s2s = s2.reshape(nslab, SUBLANE, tt)
            nsel = jnp.zeros((nslab, SUBLANE, tt), F32)
            rank = jnp.zeros((nslab, SUBLANE, tt), F32)
            for r in reversed(range(PEER_TOPK)):
                count = sum(jnp.where(c >= tau, 1.0, 0.0) for c in cand[r])
                nsel = jnp.where(s1s >= t1[r][None], count[None], nsel)
            for r in range(PEER_TOPK):
                rank = jnp.where(t2[r][None] > s2s, float(r + 1), rank)
            nsel_scr[h] = nsel.reshape(N_KEYS, tt)
            f1_scr[h] = jnp.exp(s1s - t1[0][None]).reshape(N_KEYS, tt)
            rank_b = rank.reshape(N_KEYS, tt).astype(BF16)
            f2_b = (jnp.exp(s2s - t2[0][None]) / z[None]).reshape(N_KEYS, tt).astype(BF16)
            for j in range(nj):
                rank_scr[h, j] = rank_b[:, j * LANE:(j + 1) * LANE]
                f2_scr[h, j] = f2_b[:, j * LANE:(j + 1) * LANE]
            return carry

        lax.fori_loop(0, PEER_HEADS, head, 0)

    def pre_activations(g):
        rows = slice(g * grp_rows, (g + 1) * grp_rows)
        pre = _nt(u_ref[rows, :], hn_scr[...])
        for j in range(nj):
            pre_scr[g % 2, j] = pre[:, j * LANE:(j + 1) * LANE]

    def gated_activations(g):
        for al in range(PEER_GROUP):
            a = e * (e_blk // N_KEYS) + g * PEER_GROUP + al
            rows = slice(al * N_KEYS, (al + 1) * N_KEYS)
            n_rows = [nsel_scr[h, pl.ds(a, 1), :] for h in range(PEER_HEADS)]
            f1_rows = [f1_scr[h, pl.ds(a, 1), :] for h in range(PEER_HEADS)]
            for j in range(nj):
                ln = slice(j * LANE, (j + 1) * LANE)
                gate = jnp.zeros((N_KEYS, LANE), BF16)
                for h in range(PEER_HEADS):
                    n_row = jnp.broadcast_to(n_rows[h][:, ln], (N_KEYS, LANE)).astype(BF16)
                    f1_row = jnp.broadcast_to(f1_rows[h][:, ln], (N_KEYS, LANE)).astype(BF16)
                    gate = gate + jnp.where(rank_scr[h, j] < n_row, f2_scr[h, j], 0.0) * f1_row
                p = pre_scr[g % 2, j, rows, :]
                gelu = 0.5 * p * (1.0 + lax.erf(p * (2.0 ** -0.5)))
                act_scr[g % 2, j, rows, :] = gelu.astype(BF16) * gate

    def accumulate(g):
        cols = slice(g * grp_rows, (g + 1) * grp_rows)
        act = jnp.concatenate([act_scr[g % 2, j] for j in range(nj)], axis=1)
        acc_scr[...] += _nn(vt_ref[:, cols], act)

    n_grp = e_blk // grp_rows
    pre_activations(0)
    for g in range(n_grp):
        if g + 1 < n_grp:
            pre_activations(g + 1)
        gated_activations(g)
        if g >= 1:
            accumulate(g - 1)
    accumulate(n_grp - 1)

    @pl.when(e == pl.num_programs(1) - 1)
    def _finish():
        y = x_ref[...] + acc_scr[...].T
        if final_norm:
            y = _rms(y, gfin_ref[...])
        y_ref[...] = y


def _peer(x, g, wpqt, keys, u, vtb, gfin, final_norm, tt):
    t = x.shape[0]
    e_blk = PEER_E_BLK
    once = dict(pipeline_mode=pl.Buffered(1))
    return pl.pallas_call(
        functools.partial(_peer_kernel, final_norm=final_norm),
        grid=(t // tt, N_EXPERTS // e_blk),
        in_specs=[
            pl.BlockSpec((tt, D_MODEL), lambda i, e: (i, 0), **once),
            pl.BlockSpec((1, D_MODEL), lambda i, e: (0, 0)),
            pl.BlockSpec((2 * PEER_HEADS * N_KEYS, D_MODEL), lambda i, e: (0, 0), **once),
            pl.BlockSpec((2, N_KEYS, N_KEYS), lambda i, e: (0, 0, 0)),
            pl.BlockSpec((e_blk, D_MODEL), lambda i, e: (e, 0)),
            pl.BlockSpec((None, D_MODEL, e_blk), lambda i, e: (e, 0, 0)),
            pl.BlockSpec((1, D_MODEL), lambda i, e: (0, 0)),
        ],
        out_specs=pl.BlockSpec((tt, D_MODEL), lambda i, e: (i, 0), **once),
        out_shape=jax.ShapeDtypeStruct((t, D_MODEL), F32),
        scratch_shapes=[
            pltpu.VMEM((tt, D_MODEL), BF16),
            pltpu.VMEM((PEER_HEADS, tt // LANE, N_KEYS, LANE), BF16),
            pltpu.VMEM((PEER_HEADS, tt // LANE, N_KEYS, LANE), BF16),
            pltpu.VMEM((PEER_HEADS, N_KEYS, tt), F32),
            pltpu.VMEM((PEER_HEADS, N_KEYS, tt), F32),
            pltpu.VMEM((D_MODEL, tt), F32),
            pltpu.VMEM((2, tt // LANE, PEER_GROUP * N_KEYS, LANE), F32),
            pltpu.VMEM((2, tt // LANE, PEER_GROUP * N_KEYS, LANE), BF16),
        ],
        compiler_params=_params("parallel", "arbitrary", vmem=VMEM_PHYSICAL * 15 // 16),
        name="peer",
    )(x, g, wpqt, keys, u, vtb, gfin)


def _split_w_in(w):
    sizes = (W_A, W_A, W_A, QK_L, QK_L, V_L, V_L, GATE_RANK, QK_L, QK_L, V_L, V_L)
    out, start = [], 0
    for n in sizes:
        out.append(w[:, start:start + n])
        start += n
    return out


def _layer_weights(w_in, w_gate2, b_gate, g_gla, w_out, w_pq, u_tab, v_tab):
    qa, ka, va, qb, kb, vb, rb, gb, qc, kc, vc, gc = _split_w_in(w_in)
    gb = jnp.pad(gb, ((0, 0), (0, LANE - GATE_RANK)))
    wa = jnp.concatenate([qa, ka, va], axis=1).astype(BF16)
    wl = jnp.concatenate([qb, qc, kb, kc, gb, vb, vc, rb, gc], axis=1).astype(BF16)
    wg = jnp.pad(w_gate2, ((0, LANE - GATE_RANK), (0, 0)))
    log_gamma = jnp.log(1.0 - 2.0 ** (-5.0 - jnp.arange(H_L, dtype=F32)))
    lgam = jnp.repeat(log_gamma, DK_L)
    return dict(
        wa=wa, wl=wl, wlt=wl.T, wg=wg, wgt=wg.T,
        bg=b_gate.reshape(1, QK_L), bg_col=b_gate.reshape(QK_L, 1),
        lgam=lgam.reshape(1, QK_L), lgam_col=lgam.reshape(QK_L, 1),
        ggla=jnp.tile(g_gla, H_L).reshape(1, V_L), ggla_col=g_gla.reshape(DV_L, 1),
        w_out=w_out.astype(BF16), wpqt=w_pq.T.astype(BF16),
        u=u_tab.astype(BF16),
        vtb=v_tab.reshape(N_EXPERTS // PEER_E_BLK, PEER_E_BLK, D_MODEL).transpose(0, 2, 1).astype(BF16),
    )


def kernel(x_prompt, x_sample, cache_kv_win, state_gla, state_ret, w_in, w_gate2, b_gate, g_gla,
           w_out, g_mix, g_ffn, w_pq, sub_keys, u_tab, v_tab, g_final):
    batch, seq, _ = x_prompt.shape
    n_seq, t_new, _ = x_sample.shape
    depth = w_in.shape[0]
    win = min(MAX_WINDOW, seq)
    xp = x_prompt.reshape(batch * seq, D_MODEL)
    xs = x_sample.transpose(1, 0, 2).reshape(t_new * n_seq, D_MODEL)
    tables = _sample_attn_tables(t_new)
    gfin = g_final.reshape(1, D_MODEL)
    cache_t = cache_kv_win.transpose(0, 1, 3, 4, 5, 2)
    sg = state_gla.transpose(0, 2, 3, 4, 1).reshape(depth, H_L, DK_L * DV_L, n_seq)
    sr = state_ret.transpose(0, 2, 3, 4, 1).reshape(depth, H_L, DK_L * DV_L, n_seq)
    kv_p, kv_s, gla_p, gla_s, ret_p, ret_s = [], [], [], [], [], []
    for l in range(depth):
        w = _layer_weights(w_in[l], w_gate2[l], b_gate[l], g_gla[l], w_out[l], w_pq[l], u_tab[l], v_tab[l])
        gm = g_mix[l].reshape(1, D_MODEL)
        gf = g_ffn[l].reshape(1, D_MODEL)
        last = l == depth - 1

        za, zl = _in_proj(xp, gm, w["wa"], w["wl"])
        oa = _attn_prompt(za, batch, seq)
        obc, sfin = _linattn_prompt(zl, w["wg"], w["bg"], w["lgam"], w["ggla"], batch, seq)
        xp = _out_proj_prompt(oa, obc, xp, w["w_out"])
        xp = _peer(xp, gf, w["wpqt"], sub_keys[l], w["u"], w["vtb"], gfin, last, tt=1024)
        kv = za.reshape(batch, seq, 3, H_A, HEAD_DIM)[:, seq - win:, 1:3]
        kv_p.append(kv)
        gla_p.append(sfin[:, 0])
        ret_p.append(sfin[:, 1])

        za_s, zlt = _in_proj_sample(xs, gm, w["wa"], w["wlt"])
        oa_s = _attn_sample(za_s, cache_t, l, tables, n_seq, t_new)
        obct, s1 = _linattn_sample(zlt, sg, sr, l, w["wgt"], w["bg_col"], w["lgam_col"], w["ggla_col"],
                                   n_seq, t_new)
        xs = _out_proj_sample(oa_s, obct, xs, w["w_out"])
        xs = _peer(xs, gf, w["wpqt"], sub_keys[l], w["u"], w["vtb"], gfin, last, tt=512)
        kv_new = za_s.reshape(t_new, n_seq, 3, H_A, HEAD_DIM)[:, :, 1:3].transpose(1, 0, 2, 3, 4)
        kv_s.append(kv_new)
        gla_s.append(s1[0])
        ret_s.append(s1[1])

    y_prompt = xp.reshape(batch, seq, D_MODEL)
    y_sample = xs.reshape(t_new, n_seq, D_MODEL).transpose(1, 0, 2)
    seq_major = lambda s: jnp.stack(s).reshape(depth, H_L, DK_L, DV_L, n_seq).transpose(0, 4, 1, 2, 3)
    return (y_prompt, y_sample, jnp.stack(kv_p), jnp.stack(kv_s), jnp.stack(gla_p), seq_major(gla_s),
            jnp.stack(ret_p), seq_major(ret_s))
```

```python
import functools
import math

import numpy as np
import jax
import jax.numpy as jnp
from jax import lax
from jax.experimental import pallas as pl
from jax.experimental.pallas import tpu as pltpu

F32 = jnp.float32
BF16 = jnp.bfloat16
HIGHEST = lax.Precision.HIGHEST

D_MODEL = 1024
HEAD_DIM = 64
H_A = 8
W_A = H_A * HEAD_DIM
BRANCHES = ((128, 1), (512, 4), (2048, 16))
BAND = 128
MAX_WINDOW = 2048
H_L = 4
DK_L = 32
DV_L = 64
QK_L = H_L * DK_L
V_L = H_L * DV_L
GATE_RANK = 16
GATE_TAU = 16.0
N_KEYS = 128
N_EXPERTS = N_KEYS * N_KEYS
PEER_HEADS = 8
PEER_TOPK = 16
NORM_EPS = 1e-6
W_LIN = 2 * 2 * QK_L + 128 + 2 * V_L + 2 * V_L

LANE = 128
SUBLANE = 8
VMEM_PHYSICAL = 64 * 1024 * 1024
VMEM_LIMIT = VMEM_PHYSICAL * 7 // 8


def _params(*sem, vmem=VMEM_LIMIT):
    return pltpu.CompilerParams(dimension_semantics=sem, vmem_limit_bytes=vmem)


def _rms(x, g):
    return x * lax.rsqrt(jnp.mean(x * x, axis=-1, keepdims=True) + NORM_EPS) * g


def _nt(a, b, **kw):
    return lax.dot_general(a, b, (((1,), (1,)), ((), ())), preferred_element_type=F32, **kw)


def _tn(a, b, **kw):
    return lax.dot_general(a, b, (((0,), (0,)), ((), ())), preferred_element_type=F32, **kw)


def _nn(a, b, **kw):
    return jnp.dot(a, b, preferred_element_type=F32, **kw)


def _in_proj_kernel(x_ref, g_ref, wa_ref, wl_ref, za_ref, zl_ref):
    h = _rms(x_ref[...], g_ref[...]).astype(BF16)
    za_ref[...] = _nn(h, wa_ref[...])
    zl_ref[...] = _nn(h, wl_ref[...])


def _in_proj(x, g, wa, wl, tm=256):
    t = x.shape[0]
    return pl.pallas_call(
        _in_proj_kernel,
        grid=(t // tm,),
        in_specs=[
            pl.BlockSpec((tm, D_MODEL), lambda i: (i, 0)),
            pl.BlockSpec((1, D_MODEL), lambda i: (0, 0)),
            pl.BlockSpec((D_MODEL, 3 * W_A), lambda i: (0, 0)),
            pl.BlockSpec((D_MODEL, W_LIN), lambda i: (0, 0)),
        ],
        out_specs=[
            pl.BlockSpec((tm, 3 * W_A), lambda i: (i, 0)),
            pl.BlockSpec((tm, W_LIN), lambda i: (i, 0)),
        ],
        out_shape=[jax.ShapeDtypeStruct((t, 3 * W_A), F32), jax.ShapeDtypeStruct((t, W_LIN), F32)],
        compiler_params=_params("parallel"),
        name="in_proj",
    )(x, g, wa, wl)


def _in_proj_sample_kernel(x_ref, g_ref, wa_ref, wlt_ref, za_ref, zlt_ref):
    h = _rms(x_ref[...], g_ref[...]).astype(BF16)
    za_ref[...] = _nn(h, wa_ref[...])
    zlt_ref[...] = _nt(wlt_ref[...], h)


def _in_proj_sample(x, g, wa, wlt):
    t = x.shape[0]
    return pl.pallas_call(
        _in_proj_sample_kernel,
        out_shape=[jax.ShapeDtypeStruct((t, 3 * W_A), F32), jax.ShapeDtypeStruct((W_LIN, t), F32)],
        compiler_params=pltpu.CompilerParams(vmem_limit_bytes=VMEM_LIMIT),
        name="in_proj_sample",
    )(x, g, wa, wlt)


def _alibi_slope(h):
    return 2.0 ** (-8.0 * (h + 1) / H_A)


ATT_TILE = BAND * max(d for _, d in BRANCHES)


def _attn_prompt_kernel(slope_ref, q_ref, kc_ref, kp_ref, vc_ref, vp_ref, o_ref,
                        k_scr, v_scr, m_scr, l_scr, acc_scr):
    tile = pl.program_id(2)
    k_scr[0:ATT_TILE] = kp_ref[...]
    k_scr[ATT_TILE:2 * ATT_TILE] = kc_ref[...]
    v_scr[0:ATT_TILE] = vp_ref[...]
    v_scr[ATT_TILE:2 * ATT_TILE] = vc_ref[...]
    qi = lax.broadcasted_iota(jnp.int32, (BAND, 2 * BAND), 0)
    ki = lax.broadcasted_iota(jnp.int32, (BAND, 2 * BAND), 1)
    dist = qi + BAND - ki
    in_band = (dist >= 0) & (dist <= BAND)
    distf = dist.astype(F32)
    low = lax.broadcasted_iota(jnp.int32, (BAND, LANE), 1) < HEAD_DIM
    slopes = [slope_ref[sub][0:1, 0:1] for sub in range(2)]

    def unit(u, carry, *, d, first):
        res, blk = u % d, u // d
        start = res + d * BAND * blk
        stride = None if d == 1 else d
        rows = pl.ds(start, BAND, stride=stride)
        keys = pl.ds(ATT_TILE + start - d * BAND, 2 * BAND, stride=stride)
        q = q_ref[rows, :]
        k = k_scr[keys, :].astype(BF16)
        v = v_scr[keys, :].astype(BF16)
        first_key = jnp.where((tile == 0) & (blk == 0), BAND, 0)
        valid = in_band & (ki >= first_key)
        ms, ls, pvs = [], [], []
        for sub in range(2):
            qm = jnp.where(low if sub == 0 else jnp.logical_not(low), q, 0.0).astype(BF16)
            s = _nt(qm, k) * (HEAD_DIM ** -0.5)
            s = jnp.where(valid, s - (slopes[sub] * float(d)) * distf, -jnp.inf)
            m = jnp.max(s, axis=-1, keepdims=True)
            e = jnp.exp(s - m)
            ms.append(m)
            ls.append(jnp.sum(e, axis=-1, keepdims=True))
            pvs.append(_nn(e.astype(BF16), v))
        m_u = jnp.where(low, ms[0], ms[1])
        l_u = jnp.where(low, ls[0], ls[1])
        acc_u = jnp.where(low, pvs[0], pvs[1])
        if first:
            m_scr[rows, :] = m_u
            l_scr[rows, :] = l_u
            acc_scr[rows, :] = acc_u
        else:
            m_o = m_scr[rows, :]
            m_n = jnp.maximum(m_o, m_u)
            w_o, w_u = jnp.exp(m_o - m_n), jnp.exp(m_u - m_n)
            m_scr[rows, :] = m_n
            l_scr[rows, :] = w_o * l_scr[rows, :] + w_u * l_u
            acc_scr[rows, :] = w_o * acc_scr[rows, :] + w_u * acc_u
        return carry

    for n, (_, d) in enumerate(BRANCHES):
        lax.fori_loop(0, ATT_TILE // BAND, functools.partial(unit, d=d, first=n == 0), 0, unroll=4)
    o_ref[...] = acc_scr[...] / l_scr[...]


def _attn_prompt(za, batch, seq):
    ntile = seq // ATT_TILE
    npair = H_A // 2
    slope_tab = jnp.asarray(np.broadcast_to(
        np.array([_alibi_slope(h) for h in range(H_A)], np.float32).reshape(npair, 2, 1, 1),
        (npair, 2, SUBLANE, LANE)))
    blk = (ATT_TILE, LANE)
    prev = lambda t: jnp.maximum(t - 1, 0)
    return pl.pallas_call(
        _attn_prompt_kernel,
        grid=(batch, npair, ntile),
        in_specs=[
            pl.BlockSpec((None, 2, SUBLANE, LANE), lambda b, p, t: (p, 0, 0, 0)),
            pl.BlockSpec(blk, lambda b, p, t: (b * ntile + t, p)),
            pl.BlockSpec(blk, lambda b, p, t: (b * ntile + t, npair + p)),
            pl.BlockSpec(blk, lambda b, p, t: (b * ntile + prev(t), npair + p)),
            pl.BlockSpec(blk, lambda b, p, t: (b * ntile + t, 2 * npair + p)),
            pl.BlockSpec(blk, lambda b, p, t: (b * ntile + prev(t), 2 * npair + p)),
        ],
        out_specs=pl.BlockSpec(blk, lambda b, p, t: (b * ntile + t, p)),
        out_shape=jax.ShapeDtypeStruct((batch * seq, W_A), F32),
        scratch_shapes=[pltpu.VMEM((2 * ATT_TILE, LANE), F32), pltpu.VMEM((2 * ATT_TILE, LANE), F32),
                        pltpu.VMEM((ATT_TILE, LANE), F32), pltpu.VMEM((ATT_TILE, LANE), F32),
                        pltpu.VMEM((ATT_TILE, LANE), F32)],
        compiler_params=_params("parallel", "parallel", "arbitrary"),
        name="attn_prompt",
    )(slope_tab, za, za, za, za, za)


def _sample_attn_tables(t_new):
    n_past = MAX_WINDOW

    def entry(h, t, idx):
        delta = n_past + t - idx
        if delta < 0:
            return -np.inf
        cnt = sum(1 for (w, d) in BRANCHES if delta % d == 0 and delta <= w)
        return -_alibi_slope(h) * delta + math.log(cnt) if cnt else -np.inf

    past = np.zeros((H_A, SUBLANE, MAX_WINDOW), np.float32)
    new = np.zeros((H_A, SUBLANE, SUBLANE), np.float32)
    for h in range(H_A):
        for t in range(t_new):
            past[h, t] = [entry(h, t, idx) for idx in range(MAX_WINDOW)]
            new[h, t] = [entry(h, t, n_past + s) if s < t_new else -np.inf for s in range(SUBLANE)]
    return jnp.asarray(past), jnp.asarray(new)


def _attn_sample_kernel(za_ref, kt_ref, vt_ref, bp_ref, bn_ref, o_ref, *, t_new):
    zero_rows = jnp.zeros((SUBLANE - t_new, W_A), F32)
    rows = [za_ref[t] for t in range(t_new)]
    q8 = jnp.concatenate([r[:, 0:W_A] for r in rows] + [zero_rows], axis=0)
    k8 = jnp.concatenate([r[:, W_A:2 * W_A] for r in rows] + [zero_rows], axis=0)
    v8 = jnp.concatenate([r[:, 2 * W_A:3 * W_A] for r in rows] + [zero_rows], axis=0)
    scale = HEAD_DIM ** -0.5
    outs = []
    for h in range(H_A):
        sl = slice(h * HEAD_DIM, (h + 1) * HEAD_DIM)
        qh = q8[:, sl].astype(BF16)
        s_past = _nn(qh, kt_ref[h].astype(BF16)) * scale + bp_ref[h]
        s_new = _nt(qh, k8[:, sl].astype(BF16)) * scale + bn_ref[h]
        m = jnp.maximum(s_past.max(axis=-1, keepdims=True), s_new.max(axis=-1, keepdims=True))
        p_past = jnp.exp(s_past - m)
        p_new = jnp.exp(s_new - m)
        den = p_past.sum(axis=-1, keepdims=True) + p_new.sum(axis=-1, keepdims=True)
        o = _nt(p_past.astype(BF16), vt_ref[h].astype(BF16)) + _nn(p_new.astype(BF16), v8[:, sl].astype(BF16))
        outs.append(o / den)
    out = jnp.concatenate(outs, axis=-1)
    for t in range(t_new):
        o_ref[t] = out[t:t + 1, :]


def _attn_sample(za, cache_t, layer, tables, n_seq, t_new):
    bp, bn = tables
    zav = za.reshape(t_new, n_seq, 1, 3 * W_A)
    window = lambda kv: pl.BlockSpec((None, None, None, H_A, HEAD_DIM, MAX_WINDOW),
                                     lambda b: (layer, b, kv, 0, 0, 0))
    o = pl.pallas_call(
        functools.partial(_attn_sample_kernel, t_new=t_new),
        grid=(n_seq,),
        in_specs=[
            pl.BlockSpec((t_new, None, 1, 3 * W_A), lambda b: (0, b, 0, 0)),
            window(0), window(1),
            pl.BlockSpec(bp.shape, lambda b: (0, 0, 0)),
            pl.BlockSpec(bn.shape, lambda b: (0, 0, 0)),
        ],
        out_specs=pl.BlockSpec((t_new, None, 1, W_A), lambda b: (0, b, 0, 0)),
        out_shape=jax.ShapeDtypeStruct((t_new, n_seq, 1, W_A), F32),
        compiler_params=_params("parallel"),
        name="attn_sample",
    )(zav, cache_t, cache_t, bp, bn)
    return o.reshape(t_new * n_seq, W_A)


CHUNK = 128


def _linattn_prompt_kernel(zl_ref, wg_ref, bg_ref, lgam_ref, ggla_ref, obc_ref, sfin_ref, s_scr):
    j = pl.program_id(1)

    @pl.when(j == 0)
    def _():
        s_scr[...] = jnp.zeros_like(s_scr)

    c = CHUNK
    z = zl_ref[...]
    q = z[:, 0:2 * QK_L] * (DK_L ** -0.5)
    k = z[:, 2 * QK_L:4 * QK_L]
    gb = z[:, 4 * QK_L:4 * QK_L + LANE]
    v = z[:, 4 * QK_L + LANE:4 * QK_L + LANE + 2 * V_L]
    gates = z[:, 4 * QK_L + LANE + 2 * V_L:]

    pre = _nn(gb, wg_ref[...], precision=HIGHEST) + bg_ref[...]
    la = jnp.concatenate([jax.nn.log_sigmoid(pre) / GATE_TAU, jnp.broadcast_to(lgam_ref[...], (c, QK_L))], axis=1)
    row = lax.broadcasted_iota(jnp.int32, (c, c), 0)
    col = lax.broadcasted_iota(jnp.int32, (c, c), 1)
    causal = col <= row
    b = _nn(causal.astype(F32), la, precision=HIGHEST)
    mid = b[c // 2 - 1:c // 2, :]
    last = b[c - 1:c, :]
    qt = q * jnp.exp(b - mid)
    kt = k * jnp.exp(mid - b)
    qi = (q * jnp.exp(b)).astype(BF16)
    kh = (k * jnp.exp(last - b)).astype(BF16)
    ones = jnp.ones((c, LANE), F32)

    head_qk = lax.broadcasted_iota(jnp.int32, (c, QK_L), 1) // DK_L
    head_v = lax.broadcasted_iota(jnp.int32, (c, V_L), 1) // DV_L
    blockdiag = (lax.broadcasted_iota(jnp.int32, (QK_L, V_L), 0) // DK_L
                 == lax.broadcasted_iota(jnp.int32, (QK_L, V_L), 1) // DV_L)
    causal4 = (lax.broadcasted_iota(jnp.int32, (H_L * c, c), 1)
               <= lax.broadcasted_iota(jnp.int32, (H_L * c, c), 0) % c)
    outs = []
    for mix in range(2):
        sl = slice(mix * QK_L, (mix + 1) * QK_L)
        vm = v[:, mix * V_L:(mix + 1) * V_L].astype(BF16)
        qstack = jnp.concatenate([jnp.where(head_qk == h, qt[:, sl], 0.0) for h in range(H_L)], axis=0)
        att = _nt(qstack.astype(BF16), kt[:, sl].astype(BF16))
        att = jnp.where(causal4, att, 0.0).astype(BF16)
        r = _nn(att, vm)
        o = jnp.zeros((c, V_L), F32)
        for h in range(H_L):
            o = o + jnp.where(head_v == h, r[h * c:(h + 1) * c, :], 0.0)
        s = s_scr[mix]
        o = o + _nn(qi[:, sl], s.astype(BF16))
        decay = jnp.exp(_tn(la[:, sl], ones, precision=HIGHEST))
        kv = _tn(kh[:, sl], vm)
        s_scr[mix] = jnp.concatenate([decay, decay], axis=1) * s + jnp.where(blockdiag, kv, 0.0)
        outs.append(o)

    seg = (lax.broadcasted_iota(jnp.int32, (V_L, V_L), 0) // DV_L
           == lax.broadcasted_iota(jnp.int32, (V_L, V_L), 1) // DV_L).astype(F32) * (1.0 / DV_L)
    ob = outs[0]
    ob = ob * lax.rsqrt(_nn(ob * ob, seg, precision=HIGHEST) + NORM_EPS) * ggla_ref[...]
    ob = ob * jax.nn.silu(gates[:, 0:V_L])
    oc = outs[1]
    dev = oc - _nn(oc, seg, precision=HIGHEST)
    oc = dev * lax.rsqrt(_nn(dev * dev, seg, precision=HIGHEST) + NORM_EPS) * jax.nn.silu(gates[:, V_L:2 * V_L])
    obc_ref[...] = jnp.concatenate([ob, oc], axis=1)

    @pl.when(j == pl.num_programs(1) - 1)
    def _():
        sfin_ref[...] = s_scr[...]


def _linattn_prompt(zl, wg, bg, lgam, ggla, batch, seq):
    nchunk = seq // CHUNK
    obc, sfin = pl.pallas_call(
        _linattn_prompt_kernel,
        grid=(batch, nchunk),
        in_specs=[
            pl.BlockSpec((CHUNK, W_LIN), lambda b, j: (b * nchunk + j, 0)),
            pl.BlockSpec((LANE, QK_L), lambda b, j: (0, 0)),
            pl.BlockSpec((1, QK_L), lambda b, j: (0, 0)),
            pl.BlockSpec((1, QK_L), lambda b, j: (0, 0)),
            pl.BlockSpec((1, V_L), lambda b, j: (0, 0)),
        ],
        out_specs=[
            pl.BlockSpec((CHUNK, 2 * V_L), lambda b, j: (b * nchunk + j, 0)),
            pl.BlockSpec((None, 2, QK_L, V_L), lambda b, j: (b, 0, 0, 0)),
        ],
        out_shape=[jax.ShapeDtypeStruct((batch * seq, 2 * V_L), F32),
                   jax.ShapeDtypeStruct((batch, 2, QK_L, V_L), F32)],
        scratch_shapes=[pltpu.VMEM((2, QK_L, V_L), F32)],
        compiler_params=_params("parallel", "arbitrary"),
        name="linattn_prompt",
    )(zl, wg, bg, lgam, ggla)
    sfin = sfin.reshape(batch, 2, H_L, DK_L, H_L, DV_L)
    return obc, jnp.stack([sfin[:, :, h, :, h, :] for h in range(H_L)], axis=2)


def _linattn_sample_kernel(q_ref, k_ref, gb_ref, v_ref, gate_ref, sg_ref, sr_ref, wgt_ref, bg_ref, lgam_ref,
                           ggla_ref, o_ref, s1_ref, *, n_seq, t_new):
    mix = pl.program_id(0)
    pre = _nn(wgt_ref[...], gb_ref[...], precision=HIGHEST) + bg_ref[...]
    la_gla = jax.nn.log_sigmoid(pre) / GATE_TAU
    la = jnp.where(mix == 0, la_gla, jnp.broadcast_to(lgam_ref[...], la_gla.shape))
    a = jnp.exp(la)
    q = q_ref[...] * (DK_L ** -0.5)
    k = k_ref[...]
    v = v_ref[...]
    s0 = jnp.where(mix == 0, sg_ref[...], sr_ref[...])
    s = [s0[d * DV_L:(d + 1) * DV_L, :] for d in range(DK_L)]
    outs = []
    for t in range(t_new):
        tok = slice(t * n_seq, (t + 1) * n_seq)
        vt = v[:, tok]
        ot = jnp.zeros((DV_L, n_seq), F32)
        for d in range(DK_L):
            s[d] = a[d:d + 1, tok] * s[d] + k[d:d + 1, tok] * vt
            ot = ot + q[d:d + 1, tok] * s[d]
        outs.append(ot)
    s1_ref[...] = jnp.concatenate(s, axis=0)
    o = jnp.concatenate(outs, axis=1)
    mean_sq = jnp.mean(o * o, axis=0, keepdims=True)
    o_gla = o * lax.rsqrt(mean_sq + NORM_EPS) * ggla_ref[...]
    dev = o - jnp.mean(o, axis=0, keepdims=True)
    o_ret = dev * lax.rsqrt(jnp.mean(dev * dev, axis=0, keepdims=True) + NORM_EPS)
    o_ref[...] = jnp.where(mix == 0, o_gla, o_ret) * jax.nn.silu(gate_ref[...])


def _linattn_sample(zlt, sg, sr, layer, wgt, bg_col, lgam_col, ggla_col, n_seq, t_new):
    ntok = t_new * n_seq
    state = pl.BlockSpec((None, None, DK_L * DV_L, n_seq), lambda m, h: (layer, h, 0, 0))
    q0, k0, g0, v0, r0 = 0, 2 * QK_L, 4 * QK_L, 4 * QK_L + LANE, 4 * QK_L + LANE + 2 * V_L
    o, s1 = pl.pallas_call(
        functools.partial(_linattn_sample_kernel, n_seq=n_seq, t_new=t_new),
        grid=(2, H_L),
        in_specs=[
            pl.BlockSpec((DK_L, ntok), lambda m, h: (q0 // DK_L + m * H_L + h, 0)),
            pl.BlockSpec((DK_L, ntok), lambda m, h: (k0 // DK_L + m * H_L + h, 0)),
            pl.BlockSpec((LANE, ntok), lambda m, h: (g0 // LANE, 0)),
            pl.BlockSpec((DV_L, ntok), lambda m, h: (v0 // DV_L + m * H_L + h, 0)),
            pl.BlockSpec((DV_L, ntok), lambda m, h: (r0 // DV_L + m * H_L + h, 0)),
            state, state,
            pl.BlockSpec((DK_L, LANE), lambda m, h: (h, 0)),
            pl.BlockSpec((DK_L, 1), lambda m, h: (h, 0)),
            pl.BlockSpec((DK_L, 1), lambda m, h: (h, 0)),
            pl.BlockSpec((DV_L, 1), lambda m, h: (0, 0)),
        ],
        out_specs=[
            pl.BlockSpec((DV_L, ntok), lambda m, h: (m * H_L + h, 0)),
            pl.BlockSpec((None, None, DK_L * DV_L, n_seq), lambda m, h: (m, h, 0, 0)),
        ],
        out_shape=[jax.ShapeDtypeStruct((2 * V_L, ntok), F32),
                   jax.ShapeDtypeStruct((2, H_L, DK_L * DV_L, n_seq), F32)],
        compiler_params=_params("parallel", "parallel"),
        name="linattn_sample",
    )(zlt, zlt, zlt, zlt, zlt, sg, sr, wgt, bg_col, lgam_col, ggla_col)
    return o, s1


def _out_proj_prompt_kernel(oa_ref, obc_ref, x_ref, w_ref, y_ref):
    y = _nn(oa_ref[...].astype(BF16), w_ref[0:W_A, :]) + _nn(obc_ref[...].astype(BF16), w_ref[W_A:, :])
    y_ref[...] = x_ref[...] + y


def _out_proj_prompt(oa, obc, x, w, tm=512):
    t = x.shape[0]
    half = pl.BlockSpec((tm, W_A), lambda i: (i, 0))
    full = pl.BlockSpec((tm, D_MODEL), lambda i: (i, 0))
    return pl.pallas_call(
        _out_proj_prompt_kernel,
        grid=(t // tm,),
        in_specs=[half, half, full, pl.BlockSpec((D_MODEL, D_MODEL), lambda i: (0, 0))],
        out_specs=full,
        out_shape=jax.ShapeDtypeStruct((t, D_MODEL), F32),
        compiler_params=_params("parallel"),
        name="out_proj_prompt",
    )(oa, obc, x, w)


def _out_proj_sample_kernel(oa_ref, obct_ref, x_ref, w_ref, y_ref):
    y = _nn(oa_ref[...].astype(BF16), w_ref[0:W_A, :]) + _tn(obct_ref[...].astype(BF16), w_ref[W_A:, :])
    y_ref[...] = x_ref[...] + y


def _out_proj_sample(oa, obct, x, w):
    return pl.pallas_call(
        _out_proj_sample_kernel,
        out_shape=jax.ShapeDtypeStruct(x.shape, F32),
        compiler_params=pltpu.CompilerParams(vmem_limit_bytes=VMEM_LIMIT),
        name="out_proj_sample",
    )(oa, obct, x, w)


def _bitonic_sort_desc(x):
    x = list(x)
    n = len(x)
    k = 2
    while k <= n:
        j = k // 2
        while j >= 1:
            for i in range(n):
                l = i ^ j
                if l > i:
                    hi, lo = jnp.maximum(x[i], x[l]), jnp.minimum(x[i], x[l])
                    x[i], x[l] = (hi, lo) if (i & k) == 0 else (lo, hi)
            j //= 2
        k *= 2
    return x


def _merge_top(a, b):
    n = len(a)
    x = [jnp.maximum(a[i], b[n - 1 - i]) for i in range(n)]
    j = n // 2
    while j >= 1:
        for i in range(n):
            l = i ^ j
            if l > i:
                x[i], x[l] = jnp.maximum(x[i], x[l]), jnp.minimum(x[i], x[l])
        j //= 2
    return x


def _top16_rows(s):
    t = s.shape[1]
    slabs = s.reshape(N_KEYS // SUBLANE, SUBLANE, t)
    x = _bitonic_sort_desc([slabs[i] for i in range(N_KEYS // SUBLANE)])
    for shift in (4, 2, 1):
        x = _merge_top(x, [pltpu.roll(xi, shift, 0) for xi in x])
    return x


def _top16_pair_sums(t1, t2):
    k = PEER_TOPK
    cand = [[t1[a] + t2[b] for b in range(k // (a + 1))] for a in range(k)]
    neg = jnp.full_like(t1[0], -jnp.inf)
    g0 = cand[0]
    g1 = _bitonic_sort_desc(cand[1] + cand[2] + cand[3][0:3])
    g2 = _bitonic_sort_desc(cand[3][3:4] + cand[4] + cand[5] + cand[6] + cand[7] + [cand[a][0] for a in range(8, 14)])
    g3 = [jnp.maximum(cand[14][0], cand[15][0]), jnp.minimum(cand[14][0], cand[15][0])] + [neg] * (k - 2)
    return _merge_top(_merge_top(g0, g1), _merge_top(g2, g3)), cand


PEER_GROUP = 8
PEER_E_BLK = 2048


def _peer_kernel(x_ref, g_ref, wpqt_ref, keys_ref, u_ref, vt_ref, gfin_ref, y_ref,
                 hn_scr, rank_scr, f2_scr, nsel_scr, f1_scr, acc_scr, pre_scr, act_scr, *, final_norm):
    e = pl.program_id(1)
    tt = hn_scr.shape[0]
    e_blk = u_ref.shape[0]
    nslab = N_KEYS // SUBLANE
    grp_rows = PEER_GROUP * N_KEYS
    nj = tt // LANE

    @pl.when(e == 0)
    def _route():
        hn_scr[...] = _rms(x_ref[...], g_ref[...]).astype(BF16)
        acc_scr[...] = jnp.zeros_like(acc_scr)

        def head(h, carry):
            w_rows = pl.ds(pl.multiple_of(h * 2 * N_KEYS, 2 * N_KEYS), 2 * N_KEYS)
            qk = _nt(wpqt_ref[w_rows, :], hn_scr[...])
            s1 = _nn(keys_ref[0], qk[0:N_KEYS], precision=HIGHEST)
            s2 = _nn(keys_ref[1], qk[N_KEYS:2 * N_KEYS], precision=HIGHEST)
            t1 = _top16_rows(s1)
            t2 = _top16_rows(s2)
            top, cand = _top16_pair_sums(t1, t2)
            tau, best = top[PEER_TOPK - 1], top[0]
            z = jnp.exp(top[0] - best)
            for r in range(1, PEER_TOPK):
                z = z + jnp.exp(top[r] - best)
            s1s = s1.reshape(nslab, SUBLANE, tt)
            s2s = s2.reshape(nslab, SUBLANE, tt)
            nsel = jnp.zeros((nslab, SUBLANE, tt), F32)
            rank = jnp.zeros((nslab, SUBLANE, tt), F32)
            for r in reversed(range(PEER_TOPK)):
                count = sum(jnp.where(c >= tau, 1.0, 0.0) for c in cand[r])
                nsel = jnp.where(s1s >= t1[r][None], count[None], nsel)
            for r in range(PEER_TOPK):
                rank = jnp.where(t2[r][None] > s2s, float(r + 1), rank)
            nsel_scr[h] = nsel.reshape(N_KEYS, tt)
            f1_scr[h] = jnp.exp(s1s - t1[0][None]).reshape(N_KEYS, tt)
            rank_b = rank.reshape(N_KEYS, tt).astype(BF16)
            f2_b = (jnp.exp(s2s - t2[0][None]) / z[None]).reshape(N_KEYS, tt).astype(BF16)
            for j in range(nj):
                rank_scr[h, j] = rank_b[:, j * LANE:(j + 1) * LANE]
                f2_scr[h, j] = f2_b[:, j * LANE:(j + 1) * LANE]
            return carry

        lax.fori_loop(0, PEER_HEADS, head, 0)

    def pre_activations(g):
        rows = slice(g * grp_rows, (g + 1) * grp_rows)
        pre = _nt(u_ref[rows, :], hn_scr[...])
        for j in range(nj):
            pre_scr[g % 2, j] = pre[:, j * LANE:(j + 1) * LANE]

    def gated_activations(g):
        for al in range(PEER_GROUP):
            a = e * (e_blk // N_KEYS) + g * PEER_GROUP + al
            rows = slice(al * N_KEYS, (al + 1) * N_KEYS)
            n_rows = [nsel_scr[h, pl.ds(a, 1), :] for h in range(PEER_HEADS)]
            f1_rows = [f1_scr[h, pl.ds(a, 1), :] for h in range(PEER_HEADS)]
            for j in range(nj):
                ln = slice(j * LANE, (j + 1) * LANE)
                gate = jnp.zeros((N_KEYS, LANE), BF16)
                for h in range(PEER_HEADS):
                    n_row = jnp.broadcast_to(n_rows[h][:, ln], (N_KEYS, LANE)).astype(BF16)
                    f1_row = jnp.broadcast_to(f1_rows[h][:, ln], (N_KEYS, LANE)).astype(BF16)
                    gate = gate + jnp.where(rank_scr[h, j] < n_row, f2_scr[h, j], 0.0) * f1_row
                p = pre_scr[g % 2, j, rows, :]
                gelu = 0.5 * p * (1.0 + lax.erf(p * (2.0 ** -0.5)))
                act_scr[g % 2, j, rows, :] = gelu.astype(BF16) * gate

    def accumulate(g):
        cols = slice(g * grp_rows, (g + 1) * grp_rows)
        act = jnp.concatenate([act_scr[g % 2, j] for j in range(nj)], axis=1)
        acc_scr[...] += _nn(vt_ref[:, cols], act)

    n_grp = e_blk // grp_rows
    pre_activations(0)
    for g in range(n_grp):
        if g + 1 < n_grp:
            pre_activations(g + 1)
        gated_activations(g)
        if g >= 1:
            accumulate(g - 1)
    accumulate(n_grp - 1)

    @pl.when(e == pl.num_programs(1) - 1)
    def _finish():
        y = x_ref[...] + acc_scr[...].T
        if final_norm:
            y = _rms(y, gfin_ref[...])
        y_ref[...] = y


def _peer(x, g, wpqt, keys, u, vtb, gfin, final_norm, tt):
    t = x.shape[0]
    e_blk = PEER_E_BLK
    once = dict(pipeline_mode=pl.Buffered(1))
    return pl.pallas_call(
        functools.partial(_peer_kernel, final_norm=final_norm),
        grid=(t // tt, N_EXPERTS // e_blk),
        in_specs=[
            pl.BlockSpec((tt, D_MODEL), lambda i, e: (i, 0), **once),
            pl.BlockSpec((1, D_MODEL), lambda i, e: (0, 0)),
            pl.BlockSpec((2 * PEER_HEADS * N_KEYS, D_MODEL), lambda i, e: (0, 0), **once),
            pl.BlockSpec((2, N_KEYS, N_KEYS), lambda i, e: (0, 0, 0)),
            pl.BlockSpec((e_blk, D_MODEL), lambda i, e: (e, 0)),
            pl.BlockSpec((None, D_MODEL, e_blk), lambda i, e: (e, 0, 0)),
            pl.BlockSpec((1, D_MODEL), lambda i, e: (0, 0)),
        ],
        out_specs=pl.BlockSpec((tt, D_MODEL), lambda i, e: (i, 0), **once),
        out_shape=jax.ShapeDtypeStruct((t, D_MODEL), F32),
        scratch_shapes=[
            pltpu.VMEM((tt, D_MODEL), BF16),
            pltpu.VMEM((PEER_HEADS, tt // LANE, N_KEYS, LANE), BF16),
            pltpu.VMEM((PEER_HEADS, tt // LANE, N_KEYS, LANE), BF16),
            pltpu.VMEM((PEER_HEADS, N_KEYS, tt), F32),
            pltpu.VMEM((PEER_HEADS, N_KEYS, tt), F32),
            pltpu.VMEM((D_MODEL, tt), F32),
            pltpu.VMEM((2, tt // LANE, PEER_GROUP * N_KEYS, LANE), F32),
            pltpu.VMEM((2, tt // LANE, PEER_GROUP * N_KEYS, LANE), BF16),
        ],
        compiler_params=_params("parallel", "arbitrary", vmem=VMEM_PHYSICAL * 15 // 16),
        name="peer",
    )(x, g, wpqt, keys, u, vtb, gfin)


def _split_w_in(w):
    sizes = (W_A, W_A, W_A, QK_L, QK_L, V_L, V_L, GATE_RANK, QK_L, QK_L, V_L, V_L)
    out, start = [], 0
    for n in sizes:
        out.append(w[:, start:start + n])
        start += n
    return out


def _layer_weights(w_in, w_gate2, b_gate, g_gla, w_out, w_pq, u_tab, v_tab):
    qa, ka, va, qb, kb, vb, rb, gb, qc, kc, vc, gc = _split_w_in(w_in)
    gb = jnp.pad(gb, ((0, 0), (0, LANE - GATE_RANK)))
    wa = jnp.concatenate([qa, ka, va], axis=1).astype(BF16)
    wl = jnp.concatenate([qb, qc, kb, kc, gb, vb, vc, rb, gc], axis=1).astype(BF16)
    wg = jnp.pad(w_gate2, ((0, LANE - GATE_RANK), (0, 0)))
    log_gamma = jnp.log(1.0 - 2.0 ** (-5.0 - jnp.arange(H_L, dtype=F32)))
    lgam = jnp.repeat(log_gamma, DK_L)
    return dict(
        wa=wa, wl=wl, wlt=wl.T, wg=wg, wgt=wg.T,
        bg=b_gate.reshape(1, QK_L), bg_col=b_gate.reshape(QK_L, 1),
        lgam=lgam.reshape(1, QK_L), lgam_col=lgam.reshape(QK_L, 1),
        ggla=jnp.tile(g_gla, H_L).reshape(1, V_L), ggla_col=g_gla.reshape(DV_L, 1),
        w_out=w_out.astype(BF16), wpqt=w_pq.T.astype(BF16),
        u=u_tab.astype(BF16),
        vtb=v_tab.reshape(N_EXPERTS // PEER_E_BLK, PEER_E_BLK, D_MODEL).transpose(0, 2, 1).astype(BF16),
    )


def kernel(x_prompt, x_sample, cache_kv_win, state_gla, state_ret, w_in, w_gate2, b_gate, g_gla,
           w_out, g_mix, g_ffn, w_pq, sub_keys, u_tab, v_tab, g_final):
    batch, seq, _ = x_prompt.shape
    n_seq, t_new, _ = x_sample.shape
    depth = w_in.shape[0]
    win = min(MAX_WINDOW, seq)
    xp = x_prompt.reshape(batch * seq, D_MODEL)
    xs = x_sample.transpose(1, 0, 2).reshape(t_new * n_seq, D_MODEL)
    tables = _sample_attn_tables(t_new)
    gfin = g_final.reshape(1, D_MODEL)
    cache_t = cache_kv_win.transpose(0, 1, 3, 4, 5, 2)
    sg = state_gla.transpose(0, 2, 3, 4, 1).reshape(depth, H_L, DK_L * DV_L, n_seq)
    sr = state_ret.transpose(0, 2, 3, 4, 1).reshape(depth, H_L, DK_L * DV_L, n_seq)
    kv_p, kv_s, gla_p, gla_s, ret_p, ret_s = [], [], [], [], [], []
    for l in range(depth):
        w = _layer_weights(w_in[l], w_gate2[l], b_gate[l], g_gla[l], w_out[l], w_pq[l], u_tab[l], v_tab[l])
        gm = g_mix[l].reshape(1, D_MODEL)
        gf = g_ffn[l].reshape(1, D_MODEL)
        last = l == depth - 1

        za, zl = _in_proj(xp, gm, w["wa"], w["wl"])
        oa = _attn_prompt(za, batch, seq)
        obc, sfin = _linattn_prompt(zl, w["wg"], w["bg"], w["lgam"], w["ggla"], batch, seq)
        xp = _out_proj_prompt(oa, obc, xp, w["w_out"])
        xp = _peer(xp, gf, w["wpqt"], sub_keys[l], w["u"], w["vtb"], gfin, last, tt=512)
        kv = za.reshape(batch, seq, 3, H_A, HEAD_DIM)[:, seq - win:, 1:3]
        kv_p.append(kv)
        gla_p.append(sfin[:, 0])
        ret_p.append(sfin[:, 1])

        za_s, zlt = _in_proj_sample(xs, gm, w["wa"], w["wlt"])
        oa_s = _attn_sample(za_s, cache_t, l, tables, n_seq, t_new)
        obct, s1 = _linattn_sample(zlt, sg, sr, l, w["wgt"], w["bg_col"], w["lgam_col"], w["ggla_col"],
                                   n_seq, t_new)
        xs = _out_proj_sample(oa_s, obct, xs, w["w_out"])
        xs = _peer(xs, gf, w["wpqt"], sub_keys[l], w["u"], w["vtb"], gfin, last, tt=512)
        kv_new = za_s.reshape(t_new, n_seq, 3, H_A, HEAD_DIM)[:, :, 1:3].transpose(1, 0, 2, 3, 4)
        kv_s.append(kv_new)
        gla_s.append(s1[0])
        ret_s.append(s1[1])

    y_prompt = xp.reshape(batch, seq, D_MODEL)
    y_sample = xs.reshape(t_new, n_seq, D_MODEL).transpose(1, 0, 2)
    seq_major = lambda s: jnp.stack(s).reshape(depth, H_L, DK_L, DV_L, n_seq).transpose(0, 4, 1, 2, 3)
    return (y_prompt, y_sample, jnp.stack(kv_p), jnp.stack(kv_s), jnp.stack(gla_p), seq_major(gla_s),
            jnp.stack(ret_p), seq_major(ret_s))
```

```python
import functools
import math

import numpy as np
import jax
import jax.numpy as jnp
from jax import lax
from jax.experimental import pallas as pl
from jax.experimental.pallas import tpu as pltpu

F32 = jnp.float32
BF16 = jnp.bfloat16
HIGHEST = lax.Precision.HIGHEST

D_MODEL = 1024
HEAD_DIM = 64
H_A = 8
W_A = H_A * HEAD_DIM
BRANCHES = ((128, 1), (512, 4), (2048, 16))
BAND = 128
MAX_WINDOW = 2048
H_L = 4
DK_L = 32
DV_L = 64
QK_L = H_L * DK_L
V_L = H_L * DV_L
GATE_RANK = 16
GATE_TAU = 16.0
N_KEYS = 128
N_EXPERTS = N_KEYS * N_KEYS
PEER_HEADS = 8
PEER_TOPK = 16
NORM_EPS = 1e-6
W_LIN = 2 * 2 * QK_L + 128 + 2 * V_L + 2 * V_L

LANE = 128
SUBLANE = 8
VMEM_PHYSICAL = 64 * 1024 * 1024
VMEM_LIMIT = VMEM_PHYSICAL * 7 // 8


def _params(*sem, vmem=VMEM_LIMIT):
    return pltpu.CompilerParams(dimension_semantics=sem, vmem_limit_bytes=vmem)


def _rms(x, g):
    return x * lax.rsqrt(jnp.mean(x * x, axis=-1, keepdims=True) + NORM_EPS) * g


def _nt(a, b, **kw):
    return lax.dot_general(a, b, (((1,), (1,)), ((), ())), preferred_element_type=F32, **kw)


def _tn(a, b, **kw):
    return lax.dot_general(a, b, (((0,), (0,)), ((), ())), preferred_element_type=F32, **kw)


def _nn(a, b, **kw):
    return jnp.dot(a, b, preferred_element_type=F32, **kw)


def _in_proj_kernel(x_ref, g_ref, wa_ref, wl_ref, za_ref, zl_ref):
    h = _rms(x_ref[...], g_ref[...]).astype(BF16)
    za_ref[...] = _nn(h, wa_ref[...])
    zl_ref[...] = _nn(h, wl_ref[...])


def _in_proj(x, g, wa, wl, tm=256):
    t = x.shape[0]
    return pl.pallas_call(
        _in_proj_kernel,
        grid=(t // tm,),
        in_specs=[
            pl.BlockSpec((tm, D_MODEL), lambda i: (i, 0)),
            pl.BlockSpec((1, D_MODEL), lambda i: (0, 0)),
            pl.BlockSpec((D_MODEL, 3 * W_A), lambda i: (0, 0)),
            pl.BlockSpec((D_MODEL, W_LIN), lambda i: (0, 0)),
        ],
        out_specs=[
            pl.BlockSpec((tm, 3 * W_A), lambda i: (i, 0)),
            pl.BlockSpec((tm, W_LIN), lambda i: (i, 0)),
        ],
        out_shape=[jax.ShapeDtypeStruct((t, 3 * W_A), F32), jax.ShapeDtypeStruct((t, W_LIN), F32)],
        compiler_params=_params("parallel"),
        name="in_proj",
    )(x, g, wa, wl)


def _kv_tail_kernel(x_ref, g_ref, wkvt_ref, o_ref):
    h = _rms(x_ref[...], g_ref[...]).astype(BF16)
    o_ref[...] = _nt(wkvt_ref[...], h)


def _kv_tail(x, g, wkvt, batch, seq, win, tm=512):
    per_seq, first = seq // tm, (seq - win) // tm
    return pl.pallas_call(
        _kv_tail_kernel,
        grid=(batch, win // tm),
        in_specs=[
            pl.BlockSpec((tm, D_MODEL), lambda b, i: (b * per_seq + first + i, 0)),
            pl.BlockSpec((1, D_MODEL), lambda b, i: (0, 0)),
            pl.BlockSpec((2 * W_A, D_MODEL), lambda b, i: (0, 0)),
        ],
        out_specs=pl.BlockSpec((None, 2 * W_A, tm), lambda b, i: (b, 0, i)),
        out_shape=jax.ShapeDtypeStruct((batch, 2 * W_A, win), F32),
        compiler_params=_params("parallel", "parallel"),
        name="kv_tail",
    )(x, g, wkvt)


def _in_proj_sample_kernel(x_ref, g_ref, wa_ref, wlt_ref, za_ref, zlt_ref):
    h = _rms(x_ref[...], g_ref[...]).astype(BF16)
    za_ref[...] = _nn(h, wa_ref[...])
    zlt_ref[...] = _nt(wlt_ref[...], h)


def _in_proj_sample(x, g, wa, wlt):
    t = x.shape[0]
    return pl.pallas_call(
        _in_proj_sample_kernel,
        out_shape=[jax.ShapeDtypeStruct((t, 3 * W_A), F32), jax.ShapeDtypeStruct((W_LIN, t), F32)],
        compiler_params=pltpu.CompilerParams(vmem_limit_bytes=VMEM_LIMIT),
        name="in_proj_sample",
    )(x, g, wa, wlt)


def _alibi_slope(h):
    return 2.0 ** (-8.0 * (h + 1) / H_A)


ATT_TILE = BAND * max(d for _, d in BRANCHES)


def _attn_prompt_kernel(slope_ref, q_ref, kc_ref, kp_ref, vc_ref, vp_ref, o_ref,
                        k_scr, v_scr, m_scr, l_scr, acc_scr):
    tile = pl.program_id(2)
    k_scr[0:ATT_TILE] = kp_ref[...]
    k_scr[ATT_TILE:2 * ATT_TILE] = kc_ref[...]
    v_scr[0:ATT_TILE] = vp_ref[...]
    v_scr[ATT_TILE:2 * ATT_TILE] = vc_ref[...]
    qi = lax.broadcasted_iota(jnp.int32, (BAND, 2 * BAND), 0)
    ki = lax.broadcasted_iota(jnp.int32, (BAND, 2 * BAND), 1)
    dist = qi + BAND - ki
    in_band = (dist >= 0) & (dist <= BAND)
    distf = dist.astype(F32)
    low = lax.broadcasted_iota(jnp.int32, (BAND, LANE), 1) < HEAD_DIM
    slopes = [slope_ref[sub][0:1, 0:1] for sub in range(2)]

    def unit(u, carry, *, d, first):
        res, blk = u % d, u // d
        start = res + d * BAND * blk
        stride = None if d == 1 else d
        rows = pl.ds(start, BAND, stride=stride)
        keys = pl.ds(ATT_TILE + start - d * BAND, 2 * BAND, stride=stride)
        q = q_ref[rows, :]
        k = k_scr[keys, :].astype(BF16)
        v = v_scr[keys, :].astype(BF16)
        first_key = jnp.where((tile == 0) & (blk == 0), BAND, 0)
        valid = in_band & (ki >= first_key)
        ms, ls, pvs = [], [], []
        for sub in range(2):
            qm = jnp.where(low if sub == 0 else jnp.logical_not(low), q, 0.0).astype(BF16)
            s = _nt(qm, k) * (HEAD_DIM ** -0.5)
            s = jnp.where(valid, s - (slopes[sub] * float(d)) * distf, -jnp.inf)
            m = jnp.max(s, axis=-1, keepdims=True)
            e = jnp.exp(s - m)
            ms.append(m)
            ls.append(jnp.sum(e, axis=-1, keepdims=True))
            pvs.append(_nn(e.astype(BF16), v))
        m_u = jnp.where(low, ms[0], ms[1])
        l_u = jnp.where(low, ls[0], ls[1])
        acc_u = jnp.where(low, pvs[0], pvs[1])
        if first:
            m_scr[rows, :] = m_u
            l_scr[rows, :] = l_u
            acc_scr[rows, :] = acc_u
        else:
            m_o = m_scr[rows, :]
            m_n = jnp.maximum(m_o, m_u)
            w_o, w_u = jnp.exp(m_o - m_n), jnp.exp(m_u - m_n)
            m_scr[rows, :] = m_n
            l_scr[rows, :] = w_o * l_scr[rows, :] + w_u * l_u
            acc_scr[rows, :] = w_o * acc_scr[rows, :] + w_u * acc_u
        return carry

    for n, (_, d) in enumerate(BRANCHES):
        lax.fori_loop(0, ATT_TILE // BAND, functools.partial(unit, d=d, first=n == 0), 0, unroll=4)
    o_ref[...] = acc_scr[...] / l_scr[...]


def _attn_prompt(za, batch, seq):
    ntile = seq // ATT_TILE
    npair = H_A // 2
    slope_tab = jnp.asarray(np.broadcast_to(
        np.array([_alibi_slope(h) for h in range(H_A)], np.float32).reshape(npair, 2, 1, 1),
        (npair, 2, SUBLANE, LANE)))
    blk = (ATT_TILE, LANE)
    prev = lambda t: jnp.maximum(t - 1, 0)
    return pl.pallas_call(
        _attn_prompt_kernel,
        grid=(batch, npair, ntile),
        in_specs=[
            pl.BlockSpec((None, 2, SUBLANE, LANE), lambda b, p, t: (p, 0, 0, 0)),
            pl.BlockSpec(blk, lambda b, p, t: (b * ntile + t, p)),
            pl.BlockSpec(blk, lambda b, p, t: (b * ntile + t, npair + p)),
            pl.BlockSpec(blk, lambda b, p, t: (b * ntile + prev(t), npair + p)),
            pl.BlockSpec(blk, lambda b, p, t: (b * ntile + t, 2 * npair + p)),
            pl.BlockSpec(blk, lambda b, p, t: (b * ntile + prev(t), 2 * npair + p)),
        ],
        out_specs=pl.BlockSpec(blk, lambda b, p, t: (b * ntile + t, p)),
        out_shape=jax.ShapeDtypeStruct((batch * seq, W_A), F32),
        scratch_shapes=[pltpu.VMEM((2 * ATT_TILE, LANE), F32), pltpu.VMEM((2 * ATT_TILE, LANE), F32),
                        pltpu.VMEM((ATT_TILE, LANE), F32), pltpu.VMEM((ATT_TILE, LANE), F32),
                        pltpu.VMEM((ATT_TILE, LANE), F32)],
        compiler_params=_params("parallel", "parallel", "arbitrary"),
        name="attn_prompt",
    )(slope_tab, za, za, za, za, za)


def _sample_attn_tables(t_new):
    n_past = MAX_WINDOW

    def entry(h, t, idx):
        delta = n_past + t - idx
        if delta < 0:
            return -np.inf
        cnt = sum(1 for (w, d) in BRANCHES if delta % d == 0 and delta <= w)
        return -_alibi_slope(h) * delta + math.log(cnt) if cnt else -np.inf

    past = np.zeros((H_A, SUBLANE, MAX_WINDOW), np.float32)
    new = np.zeros((H_A, SUBLANE, SUBLANE), np.float32)
    for h in range(H_A):
        for t in range(t_new):
            past[h, t] = [entry(h, t, idx) for idx in range(MAX_WINDOW)]
            new[h, t] = [entry(h, t, n_past + s) if s < t_new else -np.inf for s in range(SUBLANE)]
    return jnp.asarray(past), jnp.asarray(new)


def _attn_sample_kernel(za_ref, kt_ref, vt_ref, bp_ref, bn_ref, o_ref, *, t_new):
    zero_rows = jnp.zeros((SUBLANE - t_new, W_A), F32)
    rows = [za_ref[t] for t in range(t_new)]
    q8 = jnp.concatenate([r[:, 0:W_A] for r in rows] + [zero_rows], axis=0)
    k8 = jnp.concatenate([r[:, W_A:2 * W_A] for r in rows] + [zero_rows], axis=0)
    v8 = jnp.concatenate([r[:, 2 * W_A:3 * W_A] for r in rows] + [zero_rows], axis=0)
    scale = HEAD_DIM ** -0.5
    outs = []
    for h in range(H_A):
        sl = slice(h * HEAD_DIM, (h + 1) * HEAD_DIM)
        qh = q8[:, sl].astype(BF16)
        s_past = _nn(qh, kt_ref[h].astype(BF16)) * scale + bp_ref[h]
        s_new = _nt(qh, k8[:, sl].astype(BF16)) * scale + bn_ref[h]
        m = jnp.maximum(s_past.max(axis=-1, keepdims=True), s_new.max(axis=-1, keepdims=True))
        p_past = jnp.exp(s_past - m)
        p_new = jnp.exp(s_new - m)
        den = p_past.sum(axis=-1, keepdims=True) + p_new.sum(axis=-1, keepdims=True)
        o = _nt(p_past.astype(BF16), vt_ref[h].astype(BF16)) + _nn(p_new.astype(BF16), v8[:, sl].astype(BF16))
        outs.append(o / den)
    out = jnp.concatenate(outs, axis=-1)
    for t in range(t_new):
        o_ref[t] = out[t:t + 1, :]


def _attn_sample(za, cache_t, layer, tables, n_seq, t_new):
    bp, bn = tables
    zav = za.reshape(t_new, n_seq, 1, 3 * W_A)
    window = lambda kv: pl.BlockSpec((None, None, None, H_A, HEAD_DIM, MAX_WINDOW),
                                     lambda b: (layer, b, kv, 0, 0, 0))
    o = pl.pallas_call(
        functools.partial(_attn_sample_kernel, t_new=t_new),
        grid=(n_seq,),
        in_specs=[
            pl.BlockSpec((t_new, None, 1, 3 * W_A), lambda b: (0, b, 0, 0)),
            window(0), window(1),
            pl.BlockSpec(bp.shape, lambda b: (0, 0, 0)),
            pl.BlockSpec(bn.shape, lambda b: (0, 0, 0)),
        ],
        out_specs=pl.BlockSpec((t_new, None, 1, W_A), lambda b: (0, b, 0, 0)),
        out_shape=jax.ShapeDtypeStruct((t_new, n_seq, 1, W_A), F32),
        compiler_params=_params("parallel"),
        name="attn_sample",
    )(zav, cache_t, cache_t, bp, bn)
    return o.reshape(t_new * n_seq, W_A)


CHUNK = 128


def _linattn_prompt_kernel(zl_ref, wg_ref, bg_ref, lgam_ref, ggla_ref, obc_ref, sfin_ref, s_scr):
    j = pl.program_id(1)

    @pl.when(j == 0)
    def _():
        s_scr[...] = jnp.zeros_like(s_scr)

    c = CHUNK
    z = zl_ref[...]
    q = z[:, 0:2 * QK_L] * (DK_L ** -0.5)
    k = z[:, 2 * QK_L:4 * QK_L]
    gb = z[:, 4 * QK_L:4 * QK_L + LANE]
    v = z[:, 4 * QK_L + LANE:4 * QK_L + LANE + 2 * V_L]
    gates = z[:, 4 * QK_L + LANE + 2 * V_L:]

    pre = _nn(gb, wg_ref[...], precision=HIGHEST) + bg_ref[...]
    la = jnp.concatenate([jax.nn.log_sigmoid(pre) / GATE_TAU, jnp.broadcast_to(lgam_ref[...], (c, QK_L))], axis=1)
    row = lax.broadcasted_iota(jnp.int32, (c, c), 0)
    col = lax.broadcasted_iota(jnp.int32, (c, c), 1)
    causal = col <= row
    b = _nn(causal.astype(F32), la, precision=HIGHEST)
    mid = b[c // 2 - 1:c // 2, :]
    last = b[c - 1:c, :]
    qt = q * jnp.exp(b - mid)
    kt = k * jnp.exp(mid - b)
    qi = (q * jnp.exp(b)).astype(BF16)
    kh = (k * jnp.exp(last - b)).astype(BF16)
    ones = jnp.ones((c, LANE), F32)

    head_qk = lax.broadcasted_iota(jnp.int32, (c, QK_L), 1) // DK_L
    head_v = lax.broadcasted_iota(jnp.int32, (c, V_L), 1) // DV_L
    blockdiag = (lax.broadcasted_iota(jnp.int32, (QK_L, V_L), 0) // DK_L
                 == lax.broadcasted_iota(jnp.int32, (QK_L, V_L), 1) // DV_L)
    causal4 = (lax.broadcasted_iota(jnp.int32, (H_L * c, c), 1)
               <= lax.broadcasted_iota(jnp.int32, (H_L * c, c), 0) % c)
    outs = []
    for mix in range(2):
        sl = slice(mix * QK_L, (mix + 1) * QK_L)
        vm = v[:, mix * V_L:(mix + 1) * V_L].astype(BF16)
        qstack = jnp.concatenate([jnp.where(head_qk == h, qt[:, sl], 0.0) for h in range(H_L)], axis=0)
        att = _nt(qstack.astype(BF16), kt[:, sl].astype(BF16))
        att = jnp.where(causal4, att, 0.0).astype(BF16)
        r = _nn(att, vm)
        o = jnp.zeros((c, V_L), F32)
        for h in range(H_L):
            o = o + jnp.where(head_v == h, r[h * c:(h + 1) * c, :], 0.0)
        s = s_scr[mix]
        o = o + _nn(qi[:, sl], s.astype(BF16))
        decay = jnp.exp(_tn(la[:, sl], ones, precision=HIGHEST))
        kv = _tn(kh[:, sl], vm)
        s_scr[mix] = jnp.concatenate([decay, decay], axis=1) * s + jnp.where(blockdiag, kv, 0.0)
        outs.append(o)

    seg = (lax.broadcasted_iota(jnp.int32, (V_L, V_L), 0) // DV_L
           == lax.broadcasted_iota(jnp.int32, (V_L, V_L), 1) // DV_L).astype(F32) * (1.0 / DV_L)
    ob = outs[0]
    ob = ob * lax.rsqrt(_nn(ob * ob, seg, precision=HIGHEST) + NORM_EPS) * ggla_ref[...]
    ob = ob * jax.nn.silu(gates[:, 0:V_L])
    oc = outs[1]
    dev = oc - _nn(oc, seg, precision=HIGHEST)
    oc = dev * lax.rsqrt(_nn(dev * dev, seg, precision=HIGHEST) + NORM_EPS) * jax.nn.silu(gates[:, V_L:2 * V_L])
    obc_ref[...] = jnp.concatenate([ob, oc], axis=1)

    @pl.when(j == pl.num_programs(1) - 1)
    def _():
        sfin_ref[...] = s_scr[...]


def _linattn_prompt(zl, wg, bg, lgam, ggla, batch, seq):
    nchunk = seq // CHUNK
    obc, sfin = pl.pallas_call(
        _linattn_prompt_kernel,
        grid=(batch, nchunk),
        in_specs=[
            pl.BlockSpec((CHUNK, W_LIN), lambda b, j: (b * nchunk + j, 0)),
            pl.BlockSpec((LANE, QK_L), lambda b, j: (0, 0)),
            pl.BlockSpec((1, QK_L), lambda b, j: (0, 0)),
            pl.BlockSpec((1, QK_L), lambda b, j: (0, 0)),
            pl.BlockSpec((1, V_L), lambda b, j: (0, 0)),
        ],
        out_specs=[
            pl.BlockSpec((CHUNK, 2 * V_L), lambda b, j: (b * nchunk + j, 0)),
            pl.BlockSpec((None, 2, QK_L, V_L), lambda b, j: (b, 0, 0, 0)),
        ],
        out_shape=[jax.ShapeDtypeStruct((batch * seq, 2 * V_L), F32),
                   jax.ShapeDtypeStruct((batch, 2, QK_L, V_L), F32)],
        scratch_shapes=[pltpu.VMEM((2, QK_L, V_L), F32)],
        compiler_params=_params("parallel", "arbitrary"),
        name="linattn_prompt",
    )(zl, wg, bg, lgam, ggla)
    sfin = sfin.reshape(batch, 2, H_L, DK_L, H_L, DV_L)
    return obc, jnp.stack([sfin[:, :, h, :, h, :] for h in range(H_L)], axis=2)


def _linattn_sample_kernel(q_ref, k_ref, gb_ref, v_ref, gate_ref, sg_ref, sr_ref, wgt_ref, bg_ref, lgam_ref,
                           ggla_ref, o_ref, s1_ref, *, n_seq, t_new):
    mix = pl.program_id(0)
    pre = _nn(wgt_ref[...], gb_ref[...], precision=HIGHEST) + bg_ref[...]
    la_gla = jax.nn.log_sigmoid(pre) / GATE_TAU
    la = jnp.where(mix == 0, la_gla, jnp.broadcast_to(lgam_ref[...], la_gla.shape))
    a = jnp.exp(la)
    q = q_ref[...] * (DK_L ** -0.5)
    k = k_ref[...]
    v = v_ref[...]
    s0 = jnp.where(mix == 0, sg_ref[...], sr_ref[...])
    s = [s0[d * DV_L:(d + 1) * DV_L, :] for d in range(DK_L)]
    outs = []
    for t in range(t_new):
        tok = slice(t * n_seq, (t + 1) * n_seq)
        vt = v[:, tok]
        ot = jnp.zeros((DV_L, n_seq), F32)
        for d in range(DK_L):
            s[d] = a[d:d + 1, tok] * s[d] + k[d:d + 1, tok] * vt
            ot = ot + q[d:d + 1, tok] * s[d]
        outs.append(ot)
    s1_ref[...] = jnp.concatenate(s, axis=0)
    o = jnp.concatenate(outs, axis=1)
    mean_sq = jnp.mean(o * o, axis=0, keepdims=True)
    o_gla = o * lax.rsqrt(mean_sq + NORM_EPS) * ggla_ref[...]
    dev = o - jnp.mean(o, axis=0, keepdims=True)
    o_ret = dev * lax.rsqrt(jnp.mean(dev * dev, axis=0, keepdims=True) + NORM_EPS)
    o_ref[...] = jnp.where(mix == 0, o_gla, o_ret) * jax.nn.silu(gate_ref[...])


def _linattn_sample(zlt, sg, sr, layer, wgt, bg_col, lgam_col, ggla_col, n_seq, t_new):
    ntok = t_new * n_seq
    state = pl.BlockSpec((None, None, DK_L * DV_L, n_seq), lambda m, h: (layer, h, 0, 0))
    q0, k0, g0, v0, r0 = 0, 2 * QK_L, 4 * QK_L, 4 * QK_L + LANE, 4 * QK_L + LANE + 2 * V_L
    o, s1 = pl.pallas_call(
        functools.partial(_linattn_sample_kernel, n_seq=n_seq, t_new=t_new),
        grid=(2, H_L),
        in_specs=[
            pl.BlockSpec((DK_L, ntok), lambda m, h: (q0 // DK_L + m * H_L + h, 0)),
            pl.BlockSpec((DK_L, ntok), lambda m, h: (k0 // DK_L + m * H_L + h, 0)),
            pl.BlockSpec((LANE, ntok), lambda m, h: (g0 // LANE, 0)),
            pl.BlockSpec((DV_L, ntok), lambda m, h: (v0 // DV_L + m * H_L + h, 0)),
            pl.BlockSpec((DV_L, ntok), lambda m, h: (r0 // DV_L + m * H_L + h, 0)),
            state, state,
            pl.BlockSpec((DK_L, LANE), lambda m, h: (h, 0)),
            pl.BlockSpec((DK_L, 1), lambda m, h: (h, 0)),
            pl.BlockSpec((DK_L, 1), lambda m, h: (h, 0)),
            pl.BlockSpec((DV_L, 1), lambda m, h: (0, 0)),
        ],
        out_specs=[
            pl.BlockSpec((DV_L, ntok), lambda m, h: (m * H_L + h, 0)),
            pl.BlockSpec((None, None, DK_L * DV_L, n_seq), lambda m, h: (m, h, 0, 0)),
        ],
        out_shape=[jax.ShapeDtypeStruct((2 * V_L, ntok), F32),
                   jax.ShapeDtypeStruct((2, H_L, DK_L * DV_L, n_seq), F32)],
        compiler_params=_params("parallel", "parallel"),
        name="linattn_sample",
    )(zlt, zlt, zlt, zlt, zlt, sg, sr, wgt, bg_col, lgam_col, ggla_col)
    return o, s1


def _out_proj_prompt_kernel(oa_ref, obc_ref, x_ref, w_ref, y_ref):
    y = _nn(oa_ref[...].astype(BF16), w_ref[0:W_A, :]) + _nn(obc_ref[...].astype(BF16), w_ref[W_A:, :])
    y_ref[...] = x_ref[...] + y


def _out_proj_prompt(oa, obc, x, w, tm=512):
    t = x.shape[0]
    half = pl.BlockSpec((tm, W_A), lambda i: (i, 0))
    full = pl.BlockSpec((tm, D_MODEL), lambda i: (i, 0))
    return pl.pallas_call(
        _out_proj_prompt_kernel,
        grid=(t // tm,),
        in_specs=[half, half, full, pl.BlockSpec((D_MODEL, D_MODEL), lambda i: (0, 0))],
        out_specs=full,
        out_shape=jax.ShapeDtypeStruct((t, D_MODEL), F32),
        compiler_params=_params("parallel"),
        name="out_proj_prompt",
    )(oa, obc, x, w)


def _out_proj_sample_kernel(oa_ref, obct_ref, x_ref, w_ref, y_ref):
    y = _nn(oa_ref[...].astype(BF16), w_ref[0:W_A, :]) + _tn(obct_ref[...].astype(BF16), w_ref[W_A:, :])
    y_ref[...] = x_ref[...] + y


def _out_proj_sample(oa, obct, x, w):
    return pl.pallas_call(
        _out_proj_sample_kernel,
        out_shape=jax.ShapeDtypeStruct(x.shape, F32),
        compiler_params=pltpu.CompilerParams(vmem_limit_bytes=VMEM_LIMIT),
        name="out_proj_sample",
    )(oa, obct, x, w)


def _bitonic_sort_desc(x):
    x = list(x)
    n = len(x)
    k = 2
    while k <= n:
        j = k // 2
        while j >= 1:
            for i in range(n):
                l = i ^ j
                if l > i:
                    hi, lo = jnp.maximum(x[i], x[l]), jnp.minimum(x[i], x[l])
                    x[i], x[l] = (hi, lo) if (i & k) == 0 else (lo, hi)
            j //= 2
        k *= 2
    return x


def _merge_top(a, b):
    n = len(a)
    x = [jnp.maximum(a[i], b[n - 1 - i]) for i in range(n)]
    j = n // 2
    while j >= 1:
        for i in range(n):
            l = i ^ j
            if l > i:
                x[i], x[l] = jnp.maximum(x[i], x[l]), jnp.minimum(x[i], x[l])
        j //= 2
    return x


def _top16_rows(s):
    t = s.shape[1]
    slabs = s.reshape(N_KEYS // SUBLANE, SUBLANE, t)
    x = _bitonic_sort_desc([slabs[i] for i in range(N_KEYS // SUBLANE)])
    for shift in (4, 2, 1):
        x = _merge_top(x, [pltpu.roll(xi, shift, 0) for xi in x])
    return x


def _top16_pair_sums(t1, t2):
    k = PEER_TOPK
    cand = [[t1[a] + t2[b] for b in range(k // (a + 1))] for a in range(k)]
    neg = jnp.full_like(t1[0], -jnp.inf)
    g0 = cand[0]
    g1 = _bitonic_sort_desc(cand[1] + cand[2] + cand[3][0:3])
    g2 = _bitonic_sort_desc(cand[3][3:4] + cand[4] + cand[5] + cand[6] + cand[7] + [cand[a][0] for a in range(8, 14)])
    g3 = [jnp.maximum(cand[14][0], cand[15][0]), jnp.minimum(cand[14][0], cand[15][0])] + [neg] * (k - 2)
    return _merge_top(_merge_top(g0, g1), _merge_top(g2, g3)), cand


PEER_GROUP = 8
PEER_E_BLK = 2048


def _peer_kernel(x_ref, g_ref, wpqt_ref, keys_ref, u_ref, vt_ref, gfin_ref, y_ref,
                 hn_scr, rank_scr, f2_scr, nsel_scr, f1_scr, acc_scr, pre_scr, act_scr, top_scr, count_scr, z_scr,
                 *, final_norm):
    e = pl.program_id(1)
    tt = hn_scr.shape[0]
    e_blk = u_ref.shape[0]
    nslab = N_KEYS // SUBLANE
    grp_rows = PEER_GROUP * N_KEYS
    nj = tt // LANE

    @pl.when(e == 0)
    def _route():
        hn_scr[...] = _rms(x_ref[...], g_ref[...]).astype(BF16)
        acc_scr[...] = jnp.zeros_like(acc_scr)

        def scores(h, carry):
            w_rows = pl.ds(pl.multiple_of(h * 2 * N_KEYS, 2 * N_KEYS), 2 * N_KEYS)
            qk = _nt(wpqt_ref[w_rows, :], hn_scr[...])
            for half, dst in ((0, f1_scr), (1, nsel_scr)):
                s = _nn(keys_ref[half], qk[half * N_KEYS:(half + 1) * N_KEYS], precision=HIGHEST)
                dst[h] = s
                for r, t in enumerate(_top16_rows(s)):
                    top_scr[half, r, pl.ds(h, 1), :] = t[0:1, :]
            return carry

        lax.fori_loop(0, PEER_HEADS, scores, 0)

        t1 = [top_scr[0, r] for r in range(PEER_TOPK)]
        t2 = [top_scr[1, r] for r in range(PEER_TOPK)]
        top, cand = _top16_pair_sums(t1, t2)
        tau = top[PEER_TOPK - 1]
        z = jnp.exp(top[0] - top[0])
        for r in range(1, PEER_TOPK):
            z = z + jnp.exp(top[r] - top[0])
        z_scr[...] = z
        for r in range(PEER_TOPK):
            count_scr[r] = sum(jnp.where(c >= tau, 1.0, 0.0) for c in cand[r])

        def gates(h, carry):
            row = lambda ref, *idx: jnp.broadcast_to(ref[idx + (pl.ds(h, 1), slice(None))], (SUBLANE, tt))[None]
            s1s = f1_scr[h].reshape(nslab, SUBLANE, tt)
            s2s = nsel_scr[h].reshape(nslab, SUBLANE, tt)
            nsel = jnp.zeros((nslab, SUBLANE, tt), F32)
            rank = jnp.zeros((nslab, SUBLANE, tt), F32)
            for r in reversed(range(PEER_TOPK)):
                nsel = jnp.where(s1s >= row(top_scr, 0, r), row(count_scr, r), nsel)
            for r in range(PEER_TOPK):
                rank = jnp.where(row(top_scr, 1, r) > s2s, float(r + 1), rank)
            nsel_scr[h] = nsel.reshape(N_KEYS, tt)
            f1_scr[h] = jnp.exp(s1s - row(top_scr, 0, 0)).reshape(N_KEYS, tt)
            rank_b = rank.reshape(N_KEYS, tt).astype(BF16)
            f2_b = (jnp.exp(s2s - row(top_scr, 1, 0)) / row(z_scr)).reshape(N_KEYS, tt).astype(BF16)
            for j in range(nj):
                rank_scr[h, j] = rank_b[:, j * LANE:(j + 1) * LANE]
                f2_scr[h, j] = f2_b[:, j * LANE:(j + 1) * LANE]
            return carry

        lax.fori_loop(0, PEER_HEADS, gates, 0)

    def pre_activations(g):
        rows = slice(g * grp_rows, (g + 1) * grp_rows)
        pre = _nt(u_ref[rows, :], hn_scr[...])
        for j in range(nj):
            pre_scr[g % 2, j] = pre[:, j * LANE:(j + 1) * LANE]

    def gated_activations(g):
        for al in range(PEER_GROUP):
            a = e * (e_blk // N_KEYS) + g * PEER_GROUP + al
            rows = slice(al * N_KEYS, (al + 1) * N_KEYS)
            n_rows = [nsel_scr[h, pl.ds(a, 1), :] for h in range(PEER_HEADS)]
            f1_rows = [f1_scr[h, pl.ds(a, 1), :] for h in range(PEER_HEADS)]
            for j in range(nj):
                ln = slice(j * LANE, (j + 1) * LANE)
                gate = jnp.zeros((N_KEYS, LANE), BF16)
                for h in range(PEER_HEADS):
                    n_row = jnp.broadcast_to(n_rows[h][:, ln], (N_KEYS, LANE)).astype(BF16)
                    f1_row = jnp.broadcast_to(f1_rows[h][:, ln], (N_KEYS, LANE)).astype(BF16)
                    gate = gate + jnp.where(rank_scr[h, j] < n_row, f2_scr[h, j], 0.0) * f1_row
                p = pre_scr[g % 2, j, rows, :]
                gelu = 0.5 * p * (1.0 + lax.erf(p * (2.0 ** -0.5)))
                act_scr[g % 2, j, rows, :] = gelu.astype(BF16) * gate

    def accumulate(g):
        cols = slice(g * grp_rows, (g + 1) * grp_rows)
        act = jnp.concatenate([act_scr[g % 2, j] for j in range(nj)], axis=1)
        acc_scr[...] += _nn(vt_ref[:, cols], act)

    n_grp = e_blk // grp_rows
    pre_activations(0)
    for g in range(n_grp):
        if g + 1 < n_grp:
            pre_activations(g + 1)
        gated_activations(g)
        if g >= 1:
            accumulate(g - 1)
    accumulate(n_grp - 1)

    @pl.when(e == pl.num_programs(1) - 1)
    def _finish():
        y = x_ref[...] + acc_scr[...].T
        if final_norm:
            y = _rms(y, gfin_ref[...])
        y_ref[...] = y


def _peer(x, g, wpqt, keys, u, vtb, gfin, final_norm, tt):
    t = x.shape[0]
    e_blk = PEER_E_BLK
    once = dict(pipeline_mode=pl.Buffered(1))
    return pl.pallas_call(
        functools.partial(_peer_kernel, final_norm=final_norm),
        grid=(t // tt, N_EXPERTS // e_blk),
        in_specs=[
            pl.BlockSpec((tt, D_MODEL), lambda i, e: (i, 0), **once),
            pl.BlockSpec((1, D_MODEL), lambda i, e: (0, 0)),
            pl.BlockSpec((2 * PEER_HEADS * N_KEYS, D_MODEL), lambda i, e: (0, 0), **once),
            pl.BlockSpec((2, N_KEYS, N_KEYS), lambda i, e: (0, 0, 0)),
            pl.BlockSpec((e_blk, D_MODEL), lambda i, e: (e, 0)),
            pl.BlockSpec((None, D_MODEL, e_blk), lambda i, e: (e, 0, 0)),
            pl.BlockSpec((1, D_MODEL), lambda i, e: (0, 0)),
        ],
        out_specs=pl.BlockSpec((tt, D_MODEL), lambda i, e: (i, 0), **once),
        out_shape=jax.ShapeDtypeStruct((t, D_MODEL), F32),
        scratch_shapes=[
            pltpu.VMEM((tt, D_MODEL), BF16),
            pltpu.VMEM((PEER_HEADS, tt // LANE, N_KEYS, LANE), BF16),
            pltpu.VMEM((PEER_HEADS, tt // LANE, N_KEYS, LANE), BF16),
            pltpu.VMEM((PEER_HEADS, N_KEYS, tt), F32),
            pltpu.VMEM((PEER_HEADS, N_KEYS, tt), F32),
            pltpu.VMEM((D_MODEL, tt), F32),
            pltpu.VMEM((2, tt // LANE, PEER_GROUP * N_KEYS, LANE), F32),
            pltpu.VMEM((2, tt // LANE, PEER_GROUP * N_KEYS, LANE), BF16),
            pltpu.VMEM((2, PEER_TOPK, PEER_HEADS, tt), F32),
            pltpu.VMEM((PEER_TOPK, PEER_HEADS, tt), F32),
            pltpu.VMEM((PEER_HEADS, tt), F32),
        ],
        compiler_params=_params("parallel", "arbitrary", vmem=VMEM_PHYSICAL * 15 // 16),
        name="peer",
    )(x, g, wpqt, keys, u, vtb, gfin)


def _split_w_in(w):
    sizes = (W_A, W_A, W_A, QK_L, QK_L, V_L, V_L, GATE_RANK, QK_L, QK_L, V_L, V_L)
    out, start = [], 0
    for n in sizes:
        out.append(w[:, start:start + n])
        start += n
    return out


def _layer_weights(w_in, w_gate2, b_gate, g_gla, w_out, w_pq, u_tab, v_tab):
    qa, ka, va, qb, kb, vb, rb, gb, qc, kc, vc, gc = _split_w_in(w_in)
    gb = jnp.pad(gb, ((0, 0), (0, LANE - GATE_RANK)))
    wa = jnp.concatenate([qa, ka, va], axis=1).astype(BF16)
    wl = jnp.concatenate([qb, qc, kb, kc, gb, vb, vc, rb, gc], axis=1).astype(BF16)
    wg = jnp.pad(w_gate2, ((0, LANE - GATE_RANK), (0, 0)))
    log_gamma = jnp.log(1.0 - 2.0 ** (-5.0 - jnp.arange(H_L, dtype=F32)))
    lgam = jnp.repeat(log_gamma, DK_L)
    return dict(
        wa=wa, wl=wl, wlt=wl.T, wg=wg, wgt=wg.T,
        wkvt=jnp.concatenate([ka, va], axis=1).T.astype(BF16),
        bg=b_gate.reshape(1, QK_L), bg_col=b_gate.reshape(QK_L, 1),
        lgam=lgam.reshape(1, QK_L), lgam_col=lgam.reshape(QK_L, 1),
        ggla=jnp.tile(g_gla, H_L).reshape(1, V_L), ggla_col=g_gla.reshape(DV_L, 1),
        w_out=w_out.astype(BF16), wpqt=w_pq.T.astype(BF16),
        u=u_tab.astype(BF16),
        vtb=v_tab.reshape(N_EXPERTS // PEER_E_BLK, PEER_E_BLK, D_MODEL).transpose(0, 2, 1).astype(BF16),
    )


def kernel(x_prompt, x_sample, cache_kv_win, state_gla, state_ret, w_in, w_gate2, b_gate, g_gla,
           w_out, g_mix, g_ffn, w_pq, sub_keys, u_tab, v_tab, g_final):
    batch, seq, _ = x_prompt.shape
    n_seq, t_new, _ = x_sample.shape
    depth = w_in.shape[0]
    win = min(MAX_WINDOW, seq)
    xp = x_prompt.reshape(batch * seq, D_MODEL)
    xs = x_sample.transpose(1, 0, 2).reshape(t_new * n_seq, D_MODEL)
    tables = _sample_attn_tables(t_new)
    gfin = g_final.reshape(1, D_MODEL)
    cache_t = cache_kv_win.transpose(0, 1, 3, 4, 5, 2)
    sg = state_gla.transpose(0, 2, 3, 4, 1).reshape(depth, H_L, DK_L * DV_L, n_seq)
    sr = state_ret.transpose(0, 2, 3, 4, 1).reshape(depth, H_L, DK_L * DV_L, n_seq)
    kv_p, kv_s, gla_p, gla_s, ret_p, ret_s = [], [], [], [], [], []
    for l in range(depth):
        w = _layer_weights(w_in[l], w_gate2[l], b_gate[l], g_gla[l], w_out[l], w_pq[l], u_tab[l], v_tab[l])
        gm = g_mix[l].reshape(1, D_MODEL)
        gf = g_ffn[l].reshape(1, D_MODEL)
        last = l == depth - 1

        za, zl = _in_proj(xp, gm, w["wa"], w["wl"])
        kv_p.append(_kv_tail(xp, gm, w["wkvt"], batch, seq, win).reshape(batch, 2, H_A, HEAD_DIM, win))
        oa = _attn_prompt(za, batch, seq)
        obc, sfin = _linattn_prompt(zl, w["wg"], w["bg"], w["lgam"], w["ggla"], batch, seq)
        xp = _out_proj_prompt(oa, obc, xp, w["w_out"])
        xp = _peer(xp, gf, w["wpqt"], sub_keys[l], w["u"], w["vtb"], gfin, last, tt=512)
        gla_p.append(sfin[:, 0])
        ret_p.append(sfin[:, 1])

        za_s, zlt = _in_proj_sample(xs, gm, w["wa"], w["wlt"])
        oa_s = _attn_sample(za_s, cache_t, l, tables, n_seq, t_new)
        obct, s1 = _linattn_sample(zlt, sg, sr, l, w["wgt"], w["bg_col"], w["lgam_col"], w["ggla_col"],
                                   n_seq, t_new)
        xs = _out_proj_sample(oa_s, obct, xs, w["w_out"])
        xs = _peer(xs, gf, w["wpqt"], sub_keys[l], w["u"], w["vtb"], gfin, last, tt=512)
        kv_new = za_s.reshape(t_new, n_seq, 3, H_A, HEAD_DIM)[:, :, 1:3].transpose(1, 0, 2, 3, 4)
        kv_s.append(kv_new)
        gla_s.append(s1[0])
        ret_s.append(s1[1])

    y_prompt = xp.reshape(batch, seq, D_MODEL)
    y_sample = xs.reshape(t_new, n_seq, D_MODEL).transpose(1, 0, 2)
    seq_major = lambda s: jnp.stack(s).reshape(depth, H_L, DK_L, DV_L, n_seq).transpose(0, 4, 1, 2, 3)
    kv_prompt = jnp.stack(kv_p).transpose(0, 1, 5, 2, 3, 4)
    return (y_prompt, y_sample, kv_prompt, jnp.stack(kv_s), jnp.stack(gla_p), seq_major(gla_s),
            jnp.stack(ret_p), seq_major(ret_s))
```

```python
import functools
import math

import numpy as np
import jax
import jax.numpy as jnp
from jax import lax
from jax.experimental import pallas as pl
from jax.experimental.pallas import tpu as pltpu

F32 = jnp.float32
BF16 = jnp.bfloat16
HIGHEST = lax.Precision.HIGHEST

D_MODEL = 1024
HEAD_DIM = 64
H_A = 8
W_A = H_A * HEAD_DIM
BRANCHES = ((128, 1), (512, 4), (2048, 16))
BAND = 128
MAX_WINDOW = 2048
H_L = 4
DK_L = 32
DV_L = 64
QK_L = H_L * DK_L
V_L = H_L * DV_L
GATE_RANK = 16
GATE_TAU = 16.0
N_KEYS = 128
N_EXPERTS = N_KEYS * N_KEYS
PEER_HEADS = 8
PEER_TOPK = 16
NORM_EPS = 1e-6
W_LIN = 2 * 2 * QK_L + 128 + 2 * V_L + 2 * V_L

LANE = 128
SUBLANE = 8
VMEM_PHYSICAL = 64 * 1024 * 1024
VMEM_LIMIT = VMEM_PHYSICAL * 7 // 8


def _params(*sem, vmem=VMEM_LIMIT):
    return pltpu.CompilerParams(dimension_semantics=sem, vmem_limit_bytes=vmem)


def _rms(x, g):
    return x * lax.rsqrt(jnp.mean(x * x, axis=-1, keepdims=True) + NORM_EPS) * g


def _nt(a, b, **kw):
    return lax.dot_general(a, b, (((1,), (1,)), ((), ())), preferred_element_type=F32, **kw)


def _tn(a, b, **kw):
    return lax.dot_general(a, b, (((0,), (0,)), ((), ())), preferred_element_type=F32, **kw)


def _nn(a, b, **kw):
    return jnp.dot(a, b, preferred_element_type=F32, **kw)


def _in_proj_kernel(x_ref, g_ref, wa_ref, wl_ref, za_ref, zl_ref):
    h = _rms(x_ref[...], g_ref[...]).astype(BF16)
    za_ref[...] = _nn(h, wa_ref[...])
    zl_ref[...] = _nn(h, wl_ref[...])


def _in_proj(x, g, wa, wl, tm=256):
    t = x.shape[0]
    return pl.pallas_call(
        _in_proj_kernel,
        grid=(t // tm,),
        in_specs=[
            pl.BlockSpec((tm, D_MODEL), lambda i: (i, 0)),
            pl.BlockSpec((1, D_MODEL), lambda i: (0, 0)),
            pl.BlockSpec((D_MODEL, 3 * W_A), lambda i: (0, 0)),
            pl.BlockSpec((D_MODEL, W_LIN), lambda i: (0, 0)),
        ],
        out_specs=[
            pl.BlockSpec((tm, 3 * W_A), lambda i: (i, 0)),
            pl.BlockSpec((tm, W_LIN), lambda i: (i, 0)),
        ],
        out_shape=[jax.ShapeDtypeStruct((t, 3 * W_A), F32), jax.ShapeDtypeStruct((t, W_LIN), F32)],
        compiler_params=_params("parallel"),
        name="in_proj",
    )(x, g, wa, wl)


def _kv_tail_kernel(x_ref, g_ref, wkvt_ref, o_ref):
    h = _rms(x_ref[...], g_ref[...]).astype(BF16)
    o_ref[...] = _nt(wkvt_ref[...], h)


def _kv_tail(x, g, wkvt, batch, seq, win, tm=512):
    per_seq, first = seq // tm, (seq - win) // tm
    return pl.pallas_call(
        _kv_tail_kernel,
        grid=(batch, win // tm),
        in_specs=[
            pl.BlockSpec((tm, D_MODEL), lambda b, i: (b * per_seq + first + i, 0)),
            pl.BlockSpec((1, D_MODEL), lambda b, i: (0, 0)),
            pl.BlockSpec((2 * W_A, D_MODEL), lambda b, i: (0, 0)),
        ],
        out_specs=pl.BlockSpec((None, 2 * W_A, tm), lambda b, i: (b, 0, i)),
        out_shape=jax.ShapeDtypeStruct((batch, 2 * W_A, win), F32),
        compiler_params=_params("parallel", "parallel"),
        name="kv_tail",
    )(x, g, wkvt)


def _in_proj_sample_kernel(x_ref, g_ref, wa_ref, wlt_ref, za_ref, zlt_ref):
    h = _rms(x_ref[...], g_ref[...]).astype(BF16)
    za_ref[...] = _nn(h, wa_ref[...])
    zlt_ref[...] = _nt(wlt_ref[...], h)


def _in_proj_sample(x, g, wa, wlt):
    t = x.shape[0]
    return pl.pallas_call(
        _in_proj_sample_kernel,
        out_shape=[jax.ShapeDtypeStruct((t, 3 * W_A), F32), jax.ShapeDtypeStruct((W_LIN, t), F32)],
        compiler_params=pltpu.CompilerParams(vmem_limit_bytes=VMEM_LIMIT),
        name="in_proj_sample",
    )(x, g, wa, wlt)


def _alibi_slope(h):
    return 2.0 ** (-8.0 * (h + 1) / H_A)


ATT_TILE = BAND * max(d for _, d in BRANCHES)


def _attn_prompt_kernel(slope_ref, q_ref, kc_ref, kp_ref, vc_ref, vp_ref, o_ref,
                        k_scr, v_scr, m_scr, l_scr, acc_scr):
    tile = pl.program_id(2)
    k_scr[0:ATT_TILE] = kp_ref[...]
    k_scr[ATT_TILE:2 * ATT_TILE] = kc_ref[...]
    v_scr[0:ATT_TILE] = vp_ref[...]
    v_scr[ATT_TILE:2 * ATT_TILE] = vc_ref[...]
    qi = lax.broadcasted_iota(jnp.int32, (BAND, 2 * BAND), 0)
    ki = lax.broadcasted_iota(jnp.int32, (BAND, 2 * BAND), 1)
    dist = qi + BAND - ki
    in_band = (dist >= 0) & (dist <= BAND)
    distf = dist.astype(F32)
    low = lax.broadcasted_iota(jnp.int32, (BAND, LANE), 1) < HEAD_DIM
    slopes = [slope_ref[sub][0:1, 0:1] for sub in range(2)]

    def unit(u, carry, *, d, first):
        res, blk = u % d, u // d
        start = res + d * BAND * blk
        stride = None if d == 1 else d
        rows = pl.ds(start, BAND, stride=stride)
        keys = pl.ds(ATT_TILE + start - d * BAND, 2 * BAND, stride=stride)
        q = q_ref[rows, :]
        k = k_scr[keys, :].astype(BF16)
        v = v_scr[keys, :].astype(BF16)
        first_key = jnp.where((tile == 0) & (blk == 0), BAND, 0)
        valid = in_band & (ki >= first_key)
        ms, ls, pvs = [], [], []
        for sub in range(2):
            qm = jnp.where(low if sub == 0 else jnp.logical_not(low), q, 0.0).astype(BF16)
            s = _nt(qm, k) * (HEAD_DIM ** -0.5)
            s = jnp.where(valid, s - (slopes[sub] * float(d)) * distf, -jnp.inf)
            m = jnp.max(s, axis=-1, keepdims=True)
            e = jnp.exp(s - m)
            ms.append(m)
            ls.append(jnp.sum(e, axis=-1, keepdims=True))
            pvs.append(_nn(e.astype(BF16), v))
        m_u = jnp.where(low, ms[0], ms[1])
        l_u = jnp.where(low, ls[0], ls[1])
        acc_u = jnp.where(low, pvs[0], pvs[1])
        if first:
            m_scr[rows, :] = m_u
            l_scr[rows, :] = l_u
            acc_scr[rows, :] = acc_u
        else:
            m_o = m_scr[rows, :]
            m_n = jnp.maximum(m_o, m_u)
            w_o, w_u = jnp.exp(m_o - m_n), jnp.exp(m_u - m_n)
            m_scr[rows, :] = m_n
            l_scr[rows, :] = w_o * l_scr[rows, :] + w_u * l_u
            acc_scr[rows, :] = w_o * acc_scr[rows, :] + w_u * acc_u
        return carry

    for n, (_, d) in enumerate(BRANCHES):
        lax.fori_loop(0, ATT_TILE // BAND, functools.partial(unit, d=d, first=n == 0), 0, unroll=4)
    o_ref[...] = acc_scr[...] / l_scr[...]


def _attn_prompt(za, batch, seq):
    ntile = seq // ATT_TILE
    npair = H_A // 2
    slope_tab = jnp.asarray(np.broadcast_to(
        np.array([_alibi_slope(h) for h in range(H_A)], np.float32).reshape(npair, 2, 1, 1),
        (npair, 2, SUBLANE, LANE)))
    blk = (ATT_TILE, LANE)
    prev = lambda t: jnp.maximum(t - 1, 0)
    return pl.pallas_call(
        _attn_prompt_kernel,
        grid=(batch, npair, ntile),
        in_specs=[
            pl.BlockSpec((None, 2, SUBLANE, LANE), lambda b, p, t: (p, 0, 0, 0)),
            pl.BlockSpec(blk, lambda b, p, t: (b * ntile + t, p)),
            pl.BlockSpec(blk, lambda b, p, t: (b * ntile + t, npair + p)),
            pl.BlockSpec(blk, lambda b, p, t: (b * ntile + prev(t), npair + p)),
            pl.BlockSpec(blk, lambda b, p, t: (b * ntile + t, 2 * npair + p)),
            pl.BlockSpec(blk, lambda b, p, t: (b * ntile + prev(t), 2 * npair + p)),
        ],
        out_specs=pl.BlockSpec(blk, lambda b, p, t: (b * ntile + t, p)),
        out_shape=jax.ShapeDtypeStruct((batch * seq, W_A), F32),
        scratch_shapes=[pltpu.VMEM((2 * ATT_TILE, LANE), F32), pltpu.VMEM((2 * ATT_TILE, LANE), F32),
                        pltpu.VMEM((ATT_TILE, LANE), F32), pltpu.VMEM((ATT_TILE, LANE), F32),
                        pltpu.VMEM((ATT_TILE, LANE), F32)],
        compiler_params=_params("parallel", "parallel", "arbitrary"),
        name="attn_prompt",
    )(slope_tab, za, za, za, za, za)


def _sample_attn_tables(t_new):
    n_past = MAX_WINDOW

    def entry(h, t, idx):
        delta = n_past + t - idx
        if delta < 0:
            return -np.inf
        cnt = sum(1 for (w, d) in BRANCHES if delta % d == 0 and delta <= w)
        return -_alibi_slope(h) * delta + math.log(cnt) if cnt else -np.inf

    past = np.zeros((H_A, SUBLANE, MAX_WINDOW), np.float32)
    new = np.zeros((H_A, SUBLANE, SUBLANE), np.float32)
    for h in range(H_A):
        for t in range(t_new):
            past[h, t] = [entry(h, t, idx) for idx in range(MAX_WINDOW)]
            new[h, t] = [entry(h, t, n_past + s) if s < t_new else -np.inf for s in range(SUBLANE)]
    return jnp.asarray(past), jnp.asarray(new)


def _attn_sample_kernel(za_ref, kt_ref, vt_ref, bp_ref, bn_ref, o_ref, *, t_new):
    zero_rows = jnp.zeros((SUBLANE - t_new, W_A), F32)
    rows = [za_ref[t] for t in range(t_new)]
    q8 = jnp.concatenate([r[:, 0:W_A] for r in rows] + [zero_rows], axis=0)
    k8 = jnp.concatenate([r[:, W_A:2 * W_A] for r in rows] + [zero_rows], axis=0)
    v8 = jnp.concatenate([r[:, 2 * W_A:3 * W_A] for r in rows] + [zero_rows], axis=0)
    scale = HEAD_DIM ** -0.5
    outs = []
    for h in range(H_A):
        sl = slice(h * HEAD_DIM, (h + 1) * HEAD_DIM)
        qh = q8[:, sl].astype(BF16)
        s_past = _nn(qh, kt_ref[h].astype(BF16)) * scale + bp_ref[h]
        s_new = _nt(qh, k8[:, sl].astype(BF16)) * scale + bn_ref[h]
        m = jnp.maximum(s_past.max(axis=-1, keepdims=True), s_new.max(axis=-1, keepdims=True))
        p_past = jnp.exp(s_past - m)
        p_new = jnp.exp(s_new - m)
        den = p_past.sum(axis=-1, keepdims=True) + p_new.sum(axis=-1, keepdims=True)
        o = _nt(p_past.astype(BF16), vt_ref[h].astype(BF16)) + _nn(p_new.astype(BF16), v8[:, sl].astype(BF16))
        outs.append(o / den)
    out = jnp.concatenate(outs, axis=-1)
    for t in range(t_new):
        o_ref[t] = out[t:t + 1, :]


def _attn_sample(za, cache_t, layer, tables, n_seq, t_new):
    bp, bn = tables
    zav = za.reshape(t_new, n_seq, 1, 3 * W_A)
    window = lambda kv: pl.BlockSpec((None, None, None, H_A, HEAD_DIM, MAX_WINDOW),
                                     lambda b: (layer, b, kv, 0, 0, 0))
    o = pl.pallas_call(
        functools.partial(_attn_sample_kernel, t_new=t_new),
        grid=(n_seq,),
        in_specs=[
            pl.BlockSpec((t_new, None, 1, 3 * W_A), lambda b: (0, b, 0, 0)),
            window(0), window(1),
            pl.BlockSpec(bp.shape, lambda b: (0, 0, 0)),
            pl.BlockSpec(bn.shape, lambda b: (0, 0, 0)),
        ],
        out_specs=pl.BlockSpec((t_new, None, 1, W_A), lambda b: (0, b, 0, 0)),
        out_shape=jax.ShapeDtypeStruct((t_new, n_seq, 1, W_A), F32),
        compiler_params=_params("parallel"),
        name="attn_sample",
    )(zav, cache_t, cache_t, bp, bn)
    return o.reshape(t_new * n_seq, W_A)


CHUNK = 128


def _linattn_prompt_kernel(zl_ref, wg_ref, bg_ref, lgam_ref, ggla_ref, obc_ref, sfin_ref, s_scr):
    j = pl.program_id(1)

    @pl.when(j == 0)
    def _():
        s_scr[...] = jnp.zeros_like(s_scr)

    c = CHUNK
    z = zl_ref[...]
    q = z[:, 0:2 * QK_L] * (DK_L ** -0.5)
    k = z[:, 2 * QK_L:4 * QK_L]
    gb = z[:, 4 * QK_L:4 * QK_L + LANE]
    v = z[:, 4 * QK_L + LANE:4 * QK_L + LANE + 2 * V_L]
    gates = z[:, 4 * QK_L + LANE + 2 * V_L:]

    pre = _nn(gb, wg_ref[...], precision=HIGHEST) + bg_ref[...]
    la = jnp.concatenate([jax.nn.log_sigmoid(pre) / GATE_TAU, jnp.broadcast_to(lgam_ref[...], (c, QK_L))], axis=1)
    row = lax.broadcasted_iota(jnp.int32, (c, c), 0)
    col = lax.broadcasted_iota(jnp.int32, (c, c), 1)
    causal = col <= row
    b = _nn(causal.astype(F32), la, precision=HIGHEST)
    mid = b[c // 2 - 1:c // 2, :]
    last = b[c - 1:c, :]
    qt = q * jnp.exp(b - mid)
    kt = k * jnp.exp(mid - b)
    qi = (q * jnp.exp(b)).astype(BF16)
    kh = (k * jnp.exp(last - b)).astype(BF16)
    ones = jnp.ones((c, LANE), F32)

    head_qk = lax.broadcasted_iota(jnp.int32, (c, QK_L), 1) // DK_L
    head_v = lax.broadcasted_iota(jnp.int32, (c, V_L), 1) // DV_L
    blockdiag = (lax.broadcasted_iota(jnp.int32, (QK_L, V_L), 0) // DK_L
                 == lax.broadcasted_iota(jnp.int32, (QK_L, V_L), 1) // DV_L)
    causal4 = (lax.broadcasted_iota(jnp.int32, (H_L * c, c), 1)
               <= lax.broadcasted_iota(jnp.int32, (H_L * c, c), 0) % c)
    outs = []
    for mix in range(2):
        sl = slice(mix * QK_L, (mix + 1) * QK_L)
        vm = v[:, mix * V_L:(mix + 1) * V_L].astype(BF16)
        qstack = jnp.concatenate([jnp.where(head_qk == h, qt[:, sl], 0.0) for h in range(H_L)], axis=0)
        att = _nt(qstack.astype(BF16), kt[:, sl].astype(BF16))
        att = jnp.where(causal4, att, 0.0).astype(BF16)
        r = _nn(att, vm)
        o = jnp.zeros((c, V_L), F32)
        for h in range(H_L):
            o = o + jnp.where(head_v == h, r[h * c:(h + 1) * c, :], 0.0)
        s = s_scr[mix]
        o = o + _nn(qi[:, sl], s.astype(BF16))
        decay = jnp.exp(_tn(la[:, sl], ones, precision=HIGHEST))
        kv = _tn(kh[:, sl], vm)
        s_scr[mix] = jnp.concatenate([decay, decay], axis=1) * s + jnp.where(blockdiag, kv, 0.0)
        outs.append(o)

    seg = (lax.broadcasted_iota(jnp.int32, (V_L, V_L), 0) // DV_L
           == lax.broadcasted_iota(jnp.int32, (V_L, V_L), 1) // DV_L).astype(F32) * (1.0 / DV_L)
    ob = outs[0]
    ob = ob * lax.rsqrt(_nn(ob * ob, seg, precision=HIGHEST) + NORM_EPS) * ggla_ref[...]
    ob = ob * jax.nn.silu(gates[:, 0:V_L])
    oc = outs[1]
    dev = oc - _nn(oc, seg, precision=HIGHEST)
    oc = dev * lax.rsqrt(_nn(dev * dev, seg, precision=HIGHEST) + NORM_EPS) * jax.nn.silu(gates[:, V_L:2 * V_L])
    obc_ref[...] = jnp.concatenate([ob, oc], axis=1)

    @pl.when(j == pl.num_programs(1) - 1)
    def _():
        sfin_ref[...] = s_scr[...]


def _linattn_prompt(zl, wg, bg, lgam, ggla, batch, seq):
    nchunk = seq // CHUNK
    obc, sfin = pl.pallas_call(
        _linattn_prompt_kernel,
        grid=(batch, nchunk),
        in_specs=[
            pl.BlockSpec((CHUNK, W_LIN), lambda b, j: (b * nchunk + j, 0)),
            pl.BlockSpec((LANE, QK_L), lambda b, j: (0, 0)),
            pl.BlockSpec((1, QK_L), lambda b, j: (0, 0)),
            pl.BlockSpec((1, QK_L), lambda b, j: (0, 0)),
            pl.BlockSpec((1, V_L), lambda b, j: (0, 0)),
        ],
        out_specs=[
            pl.BlockSpec((CHUNK, 2 * V_L), lambda b, j: (b * nchunk + j, 0)),
            pl.BlockSpec((None, 2, QK_L, V_L), lambda b, j: (b, 0, 0, 0)),
        ],
        out_shape=[jax.ShapeDtypeStruct((batch * seq, 2 * V_L), F32),
                   jax.ShapeDtypeStruct((batch, 2, QK_L, V_L), F32)],
        scratch_shapes=[pltpu.VMEM((2, QK_L, V_L), F32)],
        compiler_params=_params("parallel", "arbitrary"),
        name="linattn_prompt",
    )(zl, wg, bg, lgam, ggla)
    sfin = sfin.reshape(batch, 2, H_L, DK_L, H_L, DV_L)
    return obc, jnp.stack([sfin[:, :, h, :, h, :] for h in range(H_L)], axis=2)


def _linattn_sample_kernel(q_ref, k_ref, gb_ref, v_ref, gate_ref, sg_ref, sr_ref, wgt_ref, bg_ref, lgam_ref,
                           ggla_ref, o_ref, s1_ref, *, n_seq, t_new):
    mix = pl.program_id(0)
    pre = _nn(wgt_ref[...], gb_ref[...], precision=HIGHEST) + bg_ref[...]
    la_gla = jax.nn.log_sigmoid(pre) / GATE_TAU
    la = jnp.where(mix == 0, la_gla, jnp.broadcast_to(lgam_ref[...], la_gla.shape))
    a = jnp.exp(la)
    q = q_ref[...] * (DK_L ** -0.5)
    k = k_ref[...]
    v = v_ref[...]
    s0 = jnp.where(mix == 0, sg_ref[...], sr_ref[...])
    s = [s0[d * DV_L:(d + 1) * DV_L, :] for d in range(DK_L)]
    outs = []
    for t in range(t_new):
        tok = slice(t * n_seq, (t + 1) * n_seq)
        vt = v[:, tok]
        ot = jnp.zeros((DV_L, n_seq), F32)
        for d in range(DK_L):
            s[d] = a[d:d + 1, tok] * s[d] + k[d:d + 1, tok] * vt
            ot = ot + q[d:d + 1, tok] * s[d]
        outs.append(ot)
    s1_ref[...] = jnp.concatenate(s, axis=0)
    o = jnp.concatenate(outs, axis=1)
    mean_sq = jnp.mean(o * o, axis=0, keepdims=True)
    o_gla = o * lax.rsqrt(mean_sq + NORM_EPS) * ggla_ref[...]
    dev = o - jnp.mean(o, axis=0, keepdims=True)
    o_ret = dev * lax.rsqrt(jnp.mean(dev * dev, axis=0, keepdims=True) + NORM_EPS)
    o_ref[...] = jnp.where(mix == 0, o_gla, o_ret) * jax.nn.silu(gate_ref[...])


def _linattn_sample(zlt, sg, sr, layer, wgt, bg_col, lgam_col, ggla_col, n_seq, t_new):
    ntok = t_new * n_seq
    state = pl.BlockSpec((None, None, DK_L * DV_L, n_seq), lambda m, h: (layer, h, 0, 0))
    q0, k0, g0, v0, r0 = 0, 2 * QK_L, 4 * QK_L, 4 * QK_L + LANE, 4 * QK_L + LANE + 2 * V_L
    o, s1 = pl.pallas_call(
        functools.partial(_linattn_sample_kernel, n_seq=n_seq, t_new=t_new),
        grid=(2, H_L),
        in_specs=[
            pl.BlockSpec((DK_L, ntok), lambda m, h: (q0 // DK_L + m * H_L + h, 0)),
            pl.BlockSpec((DK_L, ntok), lambda m, h: (k0 // DK_L + m * H_L + h, 0)),
            pl.BlockSpec((LANE, ntok), lambda m, h: (g0 // LANE, 0)),
            pl.BlockSpec((DV_L, ntok), lambda m, h: (v0 // DV_L + m * H_L + h, 0)),
            pl.BlockSpec((DV_L, ntok), lambda m, h: (r0 // DV_L + m * H_L + h, 0)),
            state, state,
            pl.BlockSpec((DK_L, LANE), lambda m, h: (h, 0)),
            pl.BlockSpec((DK_L, 1), lambda m, h: (h, 0)),
            pl.BlockSpec((DK_L, 1), lambda m, h: (h, 0)),
            pl.BlockSpec((DV_L, 1), lambda m, h: (0, 0)),
        ],
        out_specs=[
            pl.BlockSpec((DV_L, ntok), lambda m, h: (m * H_L + h, 0)),
            pl.BlockSpec((None, None, DK_L * DV_L, n_seq), lambda m, h: (m, h, 0, 0)),
        ],
        out_shape=[jax.ShapeDtypeStruct((2 * V_L, ntok), F32),
                   jax.ShapeDtypeStruct((2, H_L, DK_L * DV_L, n_seq), F32)],
        compiler_params=_params("parallel", "parallel"),
        name="linattn_sample",
    )(zlt, zlt, zlt, zlt, zlt, sg, sr, wgt, bg_col, lgam_col, ggla_col)
    return o, s1


def _out_proj_prompt_kernel(oa_ref, obc_ref, x_ref, w_ref, y_ref):
    y = _nn(oa_ref[...].astype(BF16), w_ref[0:W_A, :]) + _nn(obc_ref[...].astype(BF16), w_ref[W_A:, :])
    y_ref[...] = x_ref[...] + y


def _out_proj_prompt(oa, obc, x, w, tm=512):
    t = x.shape[0]
    half = pl.BlockSpec((tm, W_A), lambda i: (i, 0))
    full = pl.BlockSpec((tm, D_MODEL), lambda i: (i, 0))
    return pl.pallas_call(
        _out_proj_prompt_kernel,
        grid=(t // tm,),
        in_specs=[half, half, full, pl.BlockSpec((D_MODEL, D_MODEL), lambda i: (0, 0))],
        out_specs=full,
        out_shape=jax.ShapeDtypeStruct((t, D_MODEL), F32),
        compiler_params=_params("parallel"),
        name="out_proj_prompt",
    )(oa, obc, x, w)


def _out_proj_sample_kernel(oa_ref, obct_ref, x_ref, w_ref, y_ref):
    y = _nn(oa_ref[...].astype(BF16), w_ref[0:W_A, :]) + _tn(obct_ref[...].astype(BF16), w_ref[W_A:, :])
    y_ref[...] = x_ref[...] + y


def _out_proj_sample(oa, obct, x, w):
    return pl.pallas_call(
        _out_proj_sample_kernel,
        out_shape=jax.ShapeDtypeStruct(x.shape, F32),
        compiler_params=pltpu.CompilerParams(vmem_limit_bytes=VMEM_LIMIT),
        name="out_proj_sample",
    )(oa, obct, x, w)


def _bitonic_sort_desc(x):
    x = list(x)
    n = len(x)
    k = 2
    while k <= n:
        j = k // 2
        while j >= 1:
            for i in range(n):
                l = i ^ j
                if l > i:
                    hi, lo = jnp.maximum(x[i], x[l]), jnp.minimum(x[i], x[l])
                    x[i], x[l] = (hi, lo) if (i & k) == 0 else (lo, hi)
            j //= 2
        k *= 2
    return x


def _merge_top(a, b):
    n = len(a)
    x = [jnp.maximum(a[i], b[n - 1 - i]) for i in range(n)]
    j = n // 2
    while j >= 1:
        for i in range(n):
            l = i ^ j
            if l > i:
                x[i], x[l] = jnp.maximum(x[i], x[l]), jnp.minimum(x[i], x[l])
        j //= 2
    return x


def _top16_rows(s):
    t = s.shape[1]
    slabs = s.reshape(N_KEYS // SUBLANE, SUBLANE, t)
    x = _bitonic_sort_desc([slabs[i] for i in range(N_KEYS // SUBLANE)])
    for shift in (4, 2, 1):
        x = _merge_top(x, [pltpu.roll(xi, shift, 0) for xi in x])
    return x


def _top16_pair_sums(t1, t2):
    k = PEER_TOPK
    cand = [[t1[a] + t2[b] for b in range(k // (a + 1))] for a in range(k)]
    neg = jnp.full_like(t1[0], -jnp.inf)
    g0 = cand[0]
    g1 = _bitonic_sort_desc(cand[1] + cand[2] + cand[3][0:3])
    g2 = _bitonic_sort_desc(cand[3][3:4] + cand[4] + cand[5] + cand[6] + cand[7] + [cand[a][0] for a in range(8, 14)])
    g3 = [jnp.maximum(cand[14][0], cand[15][0]), jnp.minimum(cand[14][0], cand[15][0])] + [neg] * (k - 2)
    return _merge_top(_merge_top(g0, g1), _merge_top(g2, g3)), cand


PEER_GROUP = 8
PEER_E_BLK = 2048


def _peer_kernel(x_ref, g_ref, wpqt_ref, keys_ref, u_ref, vt_ref, gfin_ref, y_ref,
                 hn_scr, rank_scr, f2_scr, nsel_scr, f1_scr, acc_scr, pre_scr, act_scr, top_scr, count_scr, z_scr,
                 *, final_norm):
    e = pl.program_id(1)
    tt = hn_scr.shape[0]
    e_blk = u_ref.shape[0]
    nslab = N_KEYS // SUBLANE
    grp_rows = PEER_GROUP * N_KEYS
    nj = tt // LANE

    @pl.when(e == 0)
    def _route():
        hn_scr[...] = _rms(x_ref[...], g_ref[...]).astype(BF16)
        acc_scr[...] = jnp.zeros_like(acc_scr)

        def scores(h, carry):
            w_rows = pl.ds(pl.multiple_of(h * 2 * N_KEYS, 2 * N_KEYS), 2 * N_KEYS)
            qk = _nt(wpqt_ref[w_rows, :], hn_scr[...])
            for half, dst in ((0, f1_scr), (1, nsel_scr)):
                s = _nn(keys_ref[half], qk[half * N_KEYS:(half + 1) * N_KEYS], precision=HIGHEST)
                dst[h] = s
                for r, t in enumerate(_top16_rows(s)):
                    top_scr[half, r, pl.ds(h, 1), :] = t[0:1, :]
            return carry

        lax.fori_loop(0, PEER_HEADS, scores, 0)

        t1 = [top_scr[0, r] for r in range(PEER_TOPK)]
        t2 = [top_scr[1, r] for r in range(PEER_TOPK)]
        top, cand = _top16_pair_sums(t1, t2)
        tau = top[PEER_TOPK - 1]
        z = jnp.exp(top[0] - top[0])
        for r in range(1, PEER_TOPK):
            z = z + jnp.exp(top[r] - top[0])
        z_scr[...] = z
        for r in range(PEER_TOPK):
            count_scr[r] = sum(jnp.where(c >= tau, 1.0, 0.0) for c in cand[r])

        def gates(h, carry):
            row = lambda ref, *idx: jnp.broadcast_to(ref[idx + (pl.ds(h, 1), slice(None))], (SUBLANE, tt))[None]
            s1s = f1_scr[h].reshape(nslab, SUBLANE, tt)
            s2s = nsel_scr[h].reshape(nslab, SUBLANE, tt)
            nsel = jnp.zeros((nslab, SUBLANE, tt), F32)
            rank = jnp.zeros((nslab, SUBLANE, tt), F32)
            for r in reversed(range(PEER_TOPK)):
                nsel = jnp.where(s1s >= row(top_scr, 0, r), row(count_scr, r), nsel)
            for r in range(PEER_TOPK):
                rank = jnp.where(row(top_scr, 1, r) > s2s, float(r + 1), rank)
            nsel_scr[h] = nsel.reshape(N_KEYS, tt)
            f1_scr[h] = jnp.exp(s1s - row(top_scr, 0, 0)).reshape(N_KEYS, tt)
            rank_b = rank.reshape(N_KEYS, tt).astype(BF16)
            f2_b = (jnp.exp(s2s - row(top_scr, 1, 0)) / row(z_scr)).reshape(N_KEYS, tt).astype(BF16)
            for j in range(nj):
                rank_scr[h, j] = rank_b[:, j * LANE:(j + 1) * LANE]
                f2_scr[h, j] = f2_b[:, j * LANE:(j + 1) * LANE]
            return carry

        lax.fori_loop(0, PEER_HEADS, gates, 0)

    def pre_activations(g):
        rows = slice(g * grp_rows, (g + 1) * grp_rows)
        pre = _nt(u_ref[rows, :], hn_scr[...])
        for j in range(nj):
            pre_scr[g % 2, j] = pre[:, j * LANE:(j + 1) * LANE]

    def gated_activations(g):
        for al in range(PEER_GROUP):
            a = e * (e_blk // N_KEYS) + g * PEER_GROUP + al
            rows = slice(al * N_KEYS, (al + 1) * N_KEYS)
            n_rows = [nsel_scr[h, pl.ds(a, 1), :] for h in range(PEER_HEADS)]
            f1_rows = [f1_scr[h, pl.ds(a, 1), :] for h in range(PEER_HEADS)]
            for j in range(nj):
                ln = slice(j * LANE, (j + 1) * LANE)
                gate = jnp.zeros((N_KEYS, LANE), BF16)
                for h in range(PEER_HEADS):
                    n_row = jnp.broadcast_to(n_rows[h][:, ln], (N_KEYS, LANE)).astype(BF16)
                    f1_row = jnp.broadcast_to(f1_rows[h][:, ln], (N_KEYS, LANE)).astype(BF16)
                    gate = gate + jnp.where(rank_scr[h, j] < n_row, f2_scr[h, j], 0.0) * f1_row
                p = pre_scr[g % 2, j, rows, :]
                gelu = 0.5 * p * (1.0 + lax.erf(p * (2.0 ** -0.5)))
                act_scr[g % 2, j, rows, :] = gelu.astype(BF16) * gate

    def accumulate(g):
        cols = slice(g * grp_rows, (g + 1) * grp_rows)
        act = jnp.concatenate([act_scr[g % 2, j] for j in range(nj)], axis=1)
        acc_scr[...] += _nn(vt_ref[:, cols], act)

    n_grp = e_blk // grp_rows
    pre_activations(0)
    for g in range(n_grp):
        if g + 1 < n_grp:
            pre_activations(g + 1)
        gated_activations(g)
        if g >= 1:
            accumulate(g - 1)
    accumulate(n_grp - 1)

    @pl.when(e == pl.num_programs(1) - 1)
    def _finish():
        y = x_ref[...] + acc_scr[...].T
        if final_norm:
            y = _rms(y, gfin_ref[...])
        y_ref[...] = y


def _peer(x, g, wpqt, keys, u, vtb, layer, gfin, final_norm, tt):
    t = x.shape[0]
    e_blk = PEER_E_BLK
    once = dict(pipeline_mode=pl.Buffered(1))
    return pl.pallas_call(
        functools.partial(_peer_kernel, final_norm=final_norm),
        grid=(t // tt, N_EXPERTS // e_blk),
        in_specs=[
            pl.BlockSpec((tt, D_MODEL), lambda i, e: (i, 0), **once),
            pl.BlockSpec((1, D_MODEL), lambda i, e: (0, 0)),
            pl.BlockSpec((2 * PEER_HEADS * N_KEYS, D_MODEL), lambda i, e: (0, 0), **once),
            pl.BlockSpec((2, N_KEYS, N_KEYS), lambda i, e: (0, 0, 0)),
            pl.BlockSpec((None, e_blk, D_MODEL), lambda i, e: (layer, e, 0)),
            pl.BlockSpec((None, None, D_MODEL, e_blk), lambda i, e: (layer, e, 0, 0)),
            pl.BlockSpec((1, D_MODEL), lambda i, e: (0, 0)),
        ],
        out_specs=pl.BlockSpec((tt, D_MODEL), lambda i, e: (i, 0), **once),
        out_shape=jax.ShapeDtypeStruct((t, D_MODEL), F32),
        scratch_shapes=[
            pltpu.VMEM((tt, D_MODEL), BF16),
            pltpu.VMEM((PEER_HEADS, tt // LANE, N_KEYS, LANE), BF16),
            pltpu.VMEM((PEER_HEADS, tt // LANE, N_KEYS, LANE), BF16),
            pltpu.VMEM((PEER_HEADS, N_KEYS, tt), F32),
            pltpu.VMEM((PEER_HEADS, N_KEYS, tt), F32),
            pltpu.VMEM((D_MODEL, tt), F32),
            pltpu.VMEM((2, tt // LANE, PEER_GROUP * N_KEYS, LANE), F32),
            pltpu.VMEM((2, tt // LANE, PEER_GROUP * N_KEYS, LANE), BF16),
            pltpu.VMEM((2, PEER_TOPK, PEER_HEADS, tt), F32),
            pltpu.VMEM((PEER_TOPK, PEER_HEADS, tt), F32),
            pltpu.VMEM((PEER_HEADS, tt), F32),
        ],
        compiler_params=_params("parallel", "arbitrary", vmem=VMEM_PHYSICAL * 15 // 16),
        name="peer",
    )(x, g, wpqt, keys, u, vtb, gfin)


def _split_w_in(w):
    sizes = (W_A, W_A, W_A, QK_L, QK_L, V_L, V_L, GATE_RANK, QK_L, QK_L, V_L, V_L)
    out, start = [], 0
    for n in sizes:
        out.append(w[:, start:start + n])
        start += n
    return out


def _layer_weights(w_in, w_gate2, b_gate, g_gla, w_out, w_pq):
    qa, ka, va, qb, kb, vb, rb, gb, qc, kc, vc, gc = _split_w_in(w_in)
    gb = jnp.pad(gb, ((0, 0), (0, LANE - GATE_RANK)))
    wa = jnp.concatenate([qa, ka, va], axis=1).astype(BF16)
    wl = jnp.concatenate([qb, qc, kb, kc, gb, vb, vc, rb, gc], axis=1).astype(BF16)
    wg = jnp.pad(w_gate2, ((0, LANE - GATE_RANK), (0, 0)))
    log_gamma = jnp.log(1.0 - 2.0 ** (-5.0 - jnp.arange(H_L, dtype=F32)))
    lgam = jnp.repeat(log_gamma, DK_L)
    return dict(
        wa=wa, wl=wl, wlt=wl.T, wg=wg, wgt=wg.T,
        wkvt=jnp.concatenate([ka, va], axis=1).T.astype(BF16),
        bg=b_gate.reshape(1, QK_L), bg_col=b_gate.reshape(QK_L, 1),
        lgam=lgam.reshape(1, QK_L), lgam_col=lgam.reshape(QK_L, 1),
        ggla=jnp.tile(g_gla, H_L).reshape(1, V_L), ggla_col=g_gla.reshape(DV_L, 1),
        w_out=w_out.astype(BF16), wpqt=w_pq.T.astype(BF16),
    )


def kernel(x_prompt, x_sample, cache_kv_win, state_gla, state_ret, w_in, w_gate2, b_gate, g_gla,
           w_out, g_mix, g_ffn, w_pq, sub_keys, u_tab, v_tab, g_final):
    batch, seq, _ = x_prompt.shape
    n_seq, t_new, _ = x_sample.shape
    depth = w_in.shape[0]
    win = min(MAX_WINDOW, seq)
    xp = x_prompt.reshape(batch * seq, D_MODEL)
    xs = x_sample.transpose(1, 0, 2).reshape(t_new * n_seq, D_MODEL)
    tables = _sample_attn_tables(t_new)
    gfin = g_final.reshape(1, D_MODEL)
    cache_t = cache_kv_win.transpose(0, 1, 3, 4, 5, 2)
    sg = state_gla.transpose(0, 2, 3, 4, 1).reshape(depth, H_L, DK_L * DV_L, n_seq)
    sr = state_ret.transpose(0, 2, 3, 4, 1).reshape(depth, H_L, DK_L * DV_L, n_seq)
    u_all = u_tab.astype(BF16)
    vtb_all = (v_tab.reshape(depth, N_EXPERTS // PEER_E_BLK, PEER_E_BLK, D_MODEL)
               .transpose(0, 1, 3, 2).astype(BF16))
    kv_p, kv_s, gla_p, gla_s, ret_p, ret_s = [], [], [], [], [], []
    for l in range(depth):
        w = _layer_weights(w_in[l], w_gate2[l], b_gate[l], g_gla[l], w_out[l], w_pq[l])
        gm = g_mix[l].reshape(1, D_MODEL)
        gf = g_ffn[l].reshape(1, D_MODEL)
        last = l == depth - 1

        za, zl = _in_proj(xp, gm, w["wa"], w["wl"])
        kv_p.append(_kv_tail(xp, gm, w["wkvt"], batch, seq, win).reshape(batch, 2, H_A, HEAD_DIM, win))
        oa = _attn_prompt(za, batch, seq)
        obc, sfin = _linattn_prompt(zl, w["wg"], w["bg"], w["lgam"], w["ggla"], batch, seq)
        xp = _out_proj_prompt(oa, obc, xp, w["w_out"])
        xp = _peer(xp, gf, w["wpqt"], sub_keys[l], u_all, vtb_all, l, gfin, last, tt=512)
        gla_p.append(sfin[:, 0])
        ret_p.append(sfin[:, 1])

        za_s, zlt = _in_proj_sample(xs, gm, w["wa"], w["wlt"])
        oa_s = _attn_sample(za_s, cache_t, l, tables, n_seq, t_new)
        obct, s1 = _linattn_sample(zlt, sg, sr, l, w["wgt"], w["bg_col"], w["lgam_col"], w["ggla_col"],
                                   n_seq, t_new)
        xs = _out_proj_sample(oa_s, obct, xs, w["w_out"])
        xs = _peer(xs, gf, w["wpqt"], sub_keys[l], u_all, vtb_all, l, gfin, last, tt=512)
        kv_new = za_s.reshape(t_new, n_seq, 3, H_A, HEAD_DIM)[:, :, 1:3].transpose(1, 0, 2, 3, 4)
        kv_s.append(kv_new)
        gla_s.append(s1[0])
        ret_s.append(s1[1])

    y_prompt = xp.reshape(batch, seq, D_MODEL)
    y_sample = xs.reshape(t_new, n_seq, D_MODEL).transpose(1, 0, 2)
    seq_major = lambda s: jnp.stack(s).reshape(depth, H_L, DK_L, DV_L, n_seq).transpose(0, 4, 1, 2, 3)
    kv_prompt = jnp.stack(kv_p).transpose(0, 1, 5, 2, 3, 4)
    return (y_prompt, y_sample, kv_prompt, jnp.stack(kv_s), jnp.stack(gla_p), seq_major(gla_s),
            jnp.stack(ret_p), seq_major(ret_s))
```

```python
import functools
import math

import numpy as np
import jax
import jax.numpy as jnp
from jax import lax
from jax.experimental import pallas as pl
from jax.experimental.pallas import tpu as pltpu

F32 = jnp.float32
BF16 = jnp.bfloat16
HIGHEST = lax.Precision.HIGHEST

D_MODEL = 1024
HEAD_DIM = 64
H_A = 8
W_A = H_A * HEAD_DIM
BRANCHES = ((128, 1), (512, 4), (2048, 16))
BAND = 128
MAX_WINDOW = 2048
H_L = 4
DK_L = 32
DV_L = 64
QK_L = H_L * DK_L
V_L = H_L * DV_L
GATE_RANK = 16
GATE_TAU = 16.0
N_KEYS = 128
N_EXPERTS = N_KEYS * N_KEYS
PEER_HEADS = 8
PEER_TOPK = 16
NORM_EPS = 1e-6

LANE = 128
SUBLANE = 8
W_LIN = 2 * 2 * QK_L + LANE + 2 * V_L + 2 * V_L
VMEM_PHYSICAL = 64 * 1024 * 1024
VMEM_LIMIT = VMEM_PHYSICAL * 7 // 8


def _params(*sem, vmem=VMEM_LIMIT):
    return pltpu.CompilerParams(dimension_semantics=sem, vmem_limit_bytes=vmem)


def _rms(x, g):
    return x * lax.rsqrt(jnp.mean(x * x, axis=-1, keepdims=True) + NORM_EPS) * g


def _nt(a, b, **kw):
    return lax.dot_general(a, b, (((1,), (1,)), ((), ())), preferred_element_type=F32, **kw)


def _tn(a, b, **kw):
    return lax.dot_general(a, b, (((0,), (0,)), ((), ())), preferred_element_type=F32, **kw)


def _nn(a, b, **kw):
    return jnp.dot(a, b, preferred_element_type=F32, **kw)


def _in_proj_kernel(x_ref, g_ref, wa_ref, wl_ref, za_ref, zl_ref):
    h = _rms(x_ref[...], g_ref[...]).astype(BF16)
    za_ref[...] = _nn(h, wa_ref[...])
    zl_ref[...] = _nn(h, wl_ref[...])


def _in_proj(x, g, wa, wl, tm=256):
    t = x.shape[0]
    return pl.pallas_call(
        _in_proj_kernel,
        grid=(t // tm,),
        in_specs=[
            pl.BlockSpec((tm, D_MODEL), lambda i: (i, 0)),
            pl.BlockSpec((1, D_MODEL), lambda i: (0, 0)),
            pl.BlockSpec((D_MODEL, 3 * W_A), lambda i: (0, 0)),
            pl.BlockSpec((D_MODEL, W_LIN), lambda i: (0, 0)),
        ],
        out_specs=[
            pl.BlockSpec((tm, 3 * W_A), lambda i: (i, 0)),
            pl.BlockSpec((tm, W_LIN), lambda i: (i, 0)),
        ],
        out_shape=[jax.ShapeDtypeStruct((t, 3 * W_A), F32), jax.ShapeDtypeStruct((t, W_LIN), F32)],
        compiler_params=_params("parallel"),
        name="in_proj",
    )(x, g, wa, wl)


def _kv_tail_kernel(x_ref, g_ref, wkvt_ref, o_ref):
    h = _rms(x_ref[...], g_ref[...]).astype(BF16)
    o_ref[...] = _nt(wkvt_ref[...], h)


def _kv_tail(x, g, wkvt, batch, seq, win, tm=512):
    per_seq, first = seq // tm, (seq - win) // tm
    return pl.pallas_call(
        _kv_tail_kernel,
        grid=(batch, win // tm),
        in_specs=[
            pl.BlockSpec((tm, D_MODEL), lambda b, i: (b * per_seq + first + i, 0)),
            pl.BlockSpec((1, D_MODEL), lambda b, i: (0, 0)),
            pl.BlockSpec((2 * W_A, D_MODEL), lambda b, i: (0, 0)),
        ],
        out_specs=pl.BlockSpec((None, 2 * W_A, tm), lambda b, i: (b, 0, i)),
        out_shape=jax.ShapeDtypeStruct((batch, 2 * W_A, win), F32),
        compiler_params=_params("parallel", "parallel"),
        name="kv_tail",
    )(x, g, wkvt)


def _in_proj_sample_kernel(x_ref, g_ref, wa_ref, wlt_ref, za_ref, zlt_ref):
    h = _rms(x_ref[...], g_ref[...]).astype(BF16)
    za_ref[...] = _nn(h, wa_ref[...])
    zlt_ref[...] = _nt(wlt_ref[...], h)


def _in_proj_sample(x, g, wa, wlt):
    t = x.shape[0]
    return pl.pallas_call(
        _in_proj_sample_kernel,
        out_shape=[jax.ShapeDtypeStruct((t, 3 * W_A), F32), jax.ShapeDtypeStruct((W_LIN, t), F32)],
        compiler_params=pltpu.CompilerParams(vmem_limit_bytes=VMEM_LIMIT),
        name="in_proj_sample",
    )(x, g, wa, wlt)


def _alibi_slope(h):
    return 2.0 ** (-8.0 * (h + 1) / H_A)


ATT_TILE = BAND * max(d for _, d in BRANCHES)


def _attn_prompt_kernel(slope_ref, q_ref, kc_ref, kp_ref, vc_ref, vp_ref, o_ref,
                        k_scr, v_scr, m_scr, l_scr, acc_scr):
    tile = pl.program_id(2)
    k_scr[0:ATT_TILE] = kp_ref[...]
    k_scr[ATT_TILE:2 * ATT_TILE] = kc_ref[...]
    v_scr[0:ATT_TILE] = vp_ref[...]
    v_scr[ATT_TILE:2 * ATT_TILE] = vc_ref[...]
    qi = lax.broadcasted_iota(jnp.int32, (BAND, 2 * BAND), 0)
    ki = lax.broadcasted_iota(jnp.int32, (BAND, 2 * BAND), 1)
    dist = qi + BAND - ki
    in_band = (dist >= 0) & (dist <= BAND)
    distf = dist.astype(F32)
    low = lax.broadcasted_iota(jnp.int32, (BAND, LANE), 1) < HEAD_DIM
    slopes = [slope_ref[sub][0:1, 0:1] for sub in range(2)]

    def unit(u, carry, *, d, first):
        res, blk = u % d, u // d
        start = res + d * BAND * blk
        stride = None if d == 1 else d
        rows = pl.ds(start, BAND, stride=stride)
        keys = pl.ds(ATT_TILE + start - d * BAND, 2 * BAND, stride=stride)
        q = q_ref[rows, :]
        k = k_scr[keys, :].astype(BF16)
        v = v_scr[keys, :].astype(BF16)
        first_key = jnp.where((tile == 0) & (blk == 0), BAND, 0)
        valid = in_band & (ki >= first_key)
        ms, ls, pvs = [], [], []
        for sub in range(2):
            qm = jnp.where(low if sub == 0 else jnp.logical_not(low), q, 0.0).astype(BF16)
            s = _nt(qm, k) * (HEAD_DIM ** -0.5)
            s = jnp.where(valid, s - (slopes[sub] * float(d)) * distf, -jnp.inf)
            m = jnp.max(s, axis=-1, keepdims=True)
            e = jnp.exp(s - m)
            ms.append(m)
            ls.append(jnp.sum(e, axis=-1, keepdims=True))
            pvs.append(_nn(e.astype(BF16), v))
        m_u = jnp.where(low, ms[0], ms[1])
        l_u = jnp.where(low, ls[0], ls[1])
        acc_u = jnp.where(low, pvs[0], pvs[1])
        if first:
            m_scr[rows, :] = m_u
            l_scr[rows, :] = l_u
            acc_scr[rows, :] = acc_u
        else:
            m_o = m_scr[rows, :]
            m_n = jnp.maximum(m_o, m_u)
            w_o, w_u = jnp.exp(m_o - m_n), jnp.exp(m_u - m_n)
            m_scr[rows, :] = m_n
            l_scr[rows, :] = w_o * l_scr[rows, :] + w_u * l_u
            acc_scr[rows, :] = w_o * acc_scr[rows, :] + w_u * acc_u
        return carry

    for n, (_, d) in enumerate(BRANCHES):
        lax.fori_loop(0, ATT_TILE // BAND, functools.partial(unit, d=d, first=n == 0), 0, unroll=8)
    o_ref[...] = acc_scr[...] / l_scr[...]


def _attn_prompt(za, batch, seq):
    ntile = seq // ATT_TILE
    npair = H_A // 2
    slope_tab = jnp.asarray(np.broadcast_to(
        np.array([_alibi_slope(h) for h in range(H_A)], np.float32).reshape(npair, 2, 1, 1),
        (npair, 2, SUBLANE, LANE)))
    blk = (ATT_TILE, LANE)
    prev = lambda t: jnp.maximum(t - 1, 0)
    return pl.pallas_call(
        _attn_prompt_kernel,
        grid=(batch, npair, ntile),
        in_specs=[
            pl.BlockSpec((None, 2, SUBLANE, LANE), lambda b, p, t: (p, 0, 0, 0)),
            pl.BlockSpec(blk, lambda b, p, t: (b * ntile + t, p)),
            pl.BlockSpec(blk, lambda b, p, t: (b * ntile + t, npair + p)),
            pl.BlockSpec(blk, lambda b, p, t: (b * ntile + prev(t), npair + p)),
            pl.BlockSpec(blk, lambda b, p, t: (b * ntile + t, 2 * npair + p)),
            pl.BlockSpec(blk, lambda b, p, t: (b * ntile + prev(t), 2 * npair + p)),
        ],
        out_specs=pl.BlockSpec(blk, lambda b, p, t: (b * ntile + t, p)),
        out_shape=jax.ShapeDtypeStruct((batch * seq, W_A), F32),
        scratch_shapes=[pltpu.VMEM((2 * ATT_TILE, LANE), F32), pltpu.VMEM((2 * ATT_TILE, LANE), F32),
                        pltpu.VMEM((ATT_TILE, LANE), F32), pltpu.VMEM((ATT_TILE, LANE), F32),
                        pltpu.VMEM((ATT_TILE, LANE), F32)],
        compiler_params=_params("parallel", "parallel", "arbitrary"),
        name="attn_prompt",
    )(slope_tab, za, za, za, za, za)


def _sample_attn_tables(t_new):
    n_past = MAX_WINDOW

    def entry(h, t, idx):
        delta = n_past + t - idx
        if delta < 0:
            return -np.inf
        cnt = sum(1 for (w, d) in BRANCHES if delta % d == 0 and delta <= w)
        return -_alibi_slope(h) * delta + math.log(cnt) if cnt else -np.inf

    past = np.zeros((H_A, SUBLANE, MAX_WINDOW), np.float32)
    new = np.zeros((H_A, SUBLANE, SUBLANE), np.float32)
    for h in range(H_A):
        for t in range(t_new):
            past[h, t] = [entry(h, t, idx) for idx in range(MAX_WINDOW)]
            new[h, t] = [entry(h, t, n_past + s) if s < t_new else -np.inf for s in range(SUBLANE)]
    return jnp.asarray(past), jnp.asarray(new)


def _attn_sample_kernel(za_ref, kt_ref, vt_ref, bp_ref, bn_ref, o_ref, *, t_new):
    zero_rows = jnp.zeros((SUBLANE - t_new, W_A), F32)
    rows = [za_ref[t] for t in range(t_new)]
    q8 = jnp.concatenate([r[:, 0:W_A] for r in rows] + [zero_rows], axis=0)
    k8 = jnp.concatenate([r[:, W_A:2 * W_A] for r in rows] + [zero_rows], axis=0)
    v8 = jnp.concatenate([r[:, 2 * W_A:3 * W_A] for r in rows] + [zero_rows], axis=0)
    scale = HEAD_DIM ** -0.5
    outs = []
    for h in range(H_A):
        sl = slice(h * HEAD_DIM, (h + 1) * HEAD_DIM)
        qh = q8[:, sl].astype(BF16)
        s_past = _nn(qh, kt_ref[h].astype(BF16)) * scale + bp_ref[h]
        s_new = _nt(qh, k8[:, sl].astype(BF16)) * scale + bn_ref[h]
        m = jnp.maximum(s_past.max(axis=-1, keepdims=True), s_new.max(axis=-1, keepdims=True))
        p_past = jnp.exp(s_past - m)
        p_new = jnp.exp(s_new - m)
        den = p_past.sum(axis=-1, keepdims=True) + p_new.sum(axis=-1, keepdims=True)
        o = _nt(p_past.astype(BF16), vt_ref[h].astype(BF16)) + _nn(p_new.astype(BF16), v8[:, sl].astype(BF16))
        outs.append(o / den)
    out = jnp.concatenate(outs, axis=-1)
    for t in range(t_new):
        o_ref[t] = out[t:t + 1, :]


def _attn_sample(za, cache_t, layer, tables, n_seq, t_new):
    bp, bn = tables
    zav = za.reshape(t_new, n_seq, 1, 3 * W_A)
    window = lambda kv: pl.BlockSpec((None, None, None, H_A, HEAD_DIM, MAX_WINDOW),
                                     lambda b: (layer, b, kv, 0, 0, 0))
    o = pl.pallas_call(
        functools.partial(_attn_sample_kernel, t_new=t_new),
        grid=(n_seq,),
        in_specs=[
            pl.BlockSpec((t_new, None, 1, 3 * W_A), lambda b: (0, b, 0, 0)),
            window(0), window(1),
            pl.BlockSpec(bp.shape, lambda b: (0, 0, 0)),
            pl.BlockSpec(bn.shape, lambda b: (0, 0, 0)),
        ],
        out_specs=pl.BlockSpec((t_new, None, 1, W_A), lambda b: (0, b, 0, 0)),
        out_shape=jax.ShapeDtypeStruct((t_new, n_seq, 1, W_A), F32),
        compiler_params=_params("parallel"),
        name="attn_sample",
    )(zav, cache_t, cache_t, bp, bn)
    return o.reshape(t_new * n_seq, W_A)


CHUNK = 128


def _linattn_prompt_kernel(zl_ref, wg_ref, bg_ref, lgam_ref, ggla_ref, obc_ref, sfin_ref, s_scr):
    j = pl.program_id(1)

    @pl.when(j == 0)
    def _():
        s_scr[...] = jnp.zeros_like(s_scr)

    c = CHUNK
    z = zl_ref[...]
    q = z[:, 0:2 * QK_L] * (DK_L ** -0.5)
    k = z[:, 2 * QK_L:4 * QK_L]
    gb = z[:, 4 * QK_L:4 * QK_L + LANE]
    v = z[:, 4 * QK_L + LANE:4 * QK_L + LANE + 2 * V_L]
    gates = z[:, 4 * QK_L + LANE + 2 * V_L:]

    pre = _nn(gb, wg_ref[...], precision=HIGHEST) + bg_ref[...]
    la = jnp.concatenate([jax.nn.log_sigmoid(pre) / GATE_TAU, jnp.broadcast_to(lgam_ref[...], (c, QK_L))], axis=1)
    row = lax.broadcasted_iota(jnp.int32, (c, c), 0)
    col = lax.broadcasted_iota(jnp.int32, (c, c), 1)
    causal = col <= row
    b = _nn(causal.astype(F32), la, precision=HIGHEST)
    mid = b[c // 2 - 1:c // 2, :]
    last = b[c - 1:c, :]
    qt = q * jnp.exp(b - mid)
    kt = k * jnp.exp(mid - b)
    qi = (q * jnp.exp(b)).astype(BF16)
    kh = (k * jnp.exp(last - b)).astype(BF16)
    ones = jnp.ones((c, LANE), F32)

    head_qk = lax.broadcasted_iota(jnp.int32, (c, QK_L), 1) // DK_L
    head_v = lax.broadcasted_iota(jnp.int32, (c, V_L), 1) // DV_L
    blockdiag = (lax.broadcasted_iota(jnp.int32, (QK_L, V_L), 0) // DK_L
                 == lax.broadcasted_iota(jnp.int32, (QK_L, V_L), 1) // DV_L)
    causal4 = (lax.broadcasted_iota(jnp.int32, (H_L * c, c), 1)
               <= lax.broadcasted_iota(jnp.int32, (H_L * c, c), 0) % c)
    outs = []
    for mix in range(2):
        sl = slice(mix * QK_L, (mix + 1) * QK_L)
        vm = v[:, mix * V_L:(mix + 1) * V_L].astype(BF16)
        qstack = jnp.concatenate([jnp.where(head_qk == h, qt[:, sl], 0.0) for h in range(H_L)], axis=0)
        att = _nt(qstack.astype(BF16), kt[:, sl].astype(BF16))
        att = jnp.where(causal4, att, 0.0).astype(BF16)
        r = _nn(att, vm)
        o = jnp.zeros((c, V_L), F32)
        for h in range(H_L):
            o = o + jnp.where(head_v == h, r[h * c:(h + 1) * c, :], 0.0)
        s = s_scr[mix]
        o = o + _nn(qi[:, sl], s.astype(BF16))
        decay = jnp.exp(_tn(la[:, sl], ones, precision=HIGHEST))
        kv = _tn(kh[:, sl], vm)
        s_scr[mix] = jnp.concatenate([decay, decay], axis=1) * s + jnp.where(blockdiag, kv, 0.0)
        outs.append(o)

    seg = (lax.broadcasted_iota(jnp.int32, (V_L, V_L), 0) // DV_L
           == lax.broadcasted_iota(jnp.int32, (V_L, V_L), 1) // DV_L).astype(F32) * (1.0 / DV_L)
    ob = outs[0]
    ob = ob * lax.rsqrt(_nn(ob * ob, seg, precision=HIGHEST) + NORM_EPS) * ggla_ref[...]
    ob = ob * jax.nn.silu(gates[:, 0:V_L])
    oc = outs[1]
    dev = oc - _nn(oc, seg, precision=HIGHEST)
    oc = dev * lax.rsqrt(_nn(dev * dev, seg, precision=HIGHEST) + NORM_EPS) * jax.nn.silu(gates[:, V_L:2 * V_L])
    obc_ref[...] = jnp.concatenate([ob, oc], axis=1)

    @pl.when(j == pl.num_programs(1) - 1)
    def _():
        sfin_ref[...] = s_scr[...]


def _linattn_prompt(zl, wg, bg, lgam, ggla, batch, seq):
    nchunk = seq // CHUNK
    obc, sfin = pl.pallas_call(
        _linattn_prompt_kernel,
        grid=(batch, nchunk),
        in_specs=[
            pl.BlockSpec((CHUNK, W_LIN), lambda b, j: (b * nchunk + j, 0)),
            pl.BlockSpec((LANE, QK_L), lambda b, j: (0, 0)),
            pl.BlockSpec((1, QK_L), lambda b, j: (0, 0)),
            pl.BlockSpec((1, QK_L), lambda b, j: (0, 0)),
            pl.BlockSpec((1, V_L), lambda b, j: (0, 0)),
        ],
        out_specs=[
            pl.BlockSpec((CHUNK, 2 * V_L), lambda b, j: (b * nchunk + j, 0)),
            pl.BlockSpec((None, 2, QK_L, V_L), lambda b, j: (b, 0, 0, 0)),
        ],
        out_shape=[jax.ShapeDtypeStruct((batch * seq, 2 * V_L), F32),
                   jax.ShapeDtypeStruct((batch, 2, QK_L, V_L), F32)],
        scratch_shapes=[pltpu.VMEM((2, QK_L, V_L), F32)],
        compiler_params=_params("parallel", "arbitrary"),
        name="linattn_prompt",
    )(zl, wg, bg, lgam, ggla)
    sfin = sfin.reshape(batch, 2, H_L, DK_L, H_L, DV_L)
    return obc, jnp.stack([sfin[:, :, h, :, h, :] for h in range(H_L)], axis=2)


def _linattn_sample_kernel(q_ref, k_ref, gb_ref, v_ref, gate_ref, sg_ref, sr_ref, wgt_ref, bg_ref, lgam_ref,
                           ggla_ref, o_ref, s1_ref, *, n_seq, t_new):
    mix = pl.program_id(0)
    pre = _nn(wgt_ref[...], gb_ref[...], precision=HIGHEST) + bg_ref[...]
    la_gla = jax.nn.log_sigmoid(pre) / GATE_TAU
    la = jnp.where(mix == 0, la_gla, jnp.broadcast_to(lgam_ref[...], la_gla.shape))
    a = jnp.exp(la)
    q = q_ref[...] * (DK_L ** -0.5)
    k = k_ref[...]
    v = v_ref[...]
    s0 = jnp.where(mix == 0, sg_ref[...], sr_ref[...])
    s = [s0[d * DV_L:(d + 1) * DV_L, :] for d in range(DK_L)]
    outs = []
    for t in range(t_new):
        tok = slice(t * n_seq, (t + 1) * n_seq)
        vt = v[:, tok]
        ot = jnp.zeros((DV_L, n_seq), F32)
        for d in range(DK_L):
            s[d] = a[d:d + 1, tok] * s[d] + k[d:d + 1, tok] * vt
            ot = ot + q[d:d + 1, tok] * s[d]
        outs.append(ot)
    s1_ref[...] = jnp.concatenate(s, axis=0)
    o = jnp.concatenate(outs, axis=1)
    mean_sq = jnp.mean(o * o, axis=0, keepdims=True)
    o_gla = o * lax.rsqrt(mean_sq + NORM_EPS) * ggla_ref[...]
    dev = o - jnp.mean(o, axis=0, keepdims=True)
    o_ret = dev * lax.rsqrt(jnp.mean(dev * dev, axis=0, keepdims=True) + NORM_EPS)
    o_ref[...] = jnp.where(mix == 0, o_gla, o_ret) * jax.nn.silu(gate_ref[...])


def _linattn_sample(zlt, sg, sr, layer, wgt, bg_col, lgam_col, ggla_col, n_seq, t_new):
    ntok = t_new * n_seq
    state = pl.BlockSpec((None, None, DK_L * DV_L, n_seq), lambda m, h: (layer, h, 0, 0))
    q0, k0, g0, v0, r0 = 0, 2 * QK_L, 4 * QK_L, 4 * QK_L + LANE, 4 * QK_L + LANE + 2 * V_L
    o, s1 = pl.pallas_call(
        functools.partial(_linattn_sample_kernel, n_seq=n_seq, t_new=t_new),
        grid=(2, H_L),
        in_specs=[
            pl.BlockSpec((DK_L, ntok), lambda m, h: (q0 // DK_L + m * H_L + h, 0)),
            pl.BlockSpec((DK_L, ntok), lambda m, h: (k0 // DK_L + m * H_L + h, 0)),
            pl.BlockSpec((LANE, ntok), lambda m, h: (g0 // LANE, 0)),
            pl.BlockSpec((DV_L, ntok), lambda m, h: (v0 // DV_L + m * H_L + h, 0)),
            pl.BlockSpec((DV_L, ntok), lambda m, h: (r0 // DV_L + m * H_L + h, 0)),
            state, state,
            pl.BlockSpec((DK_L, LANE), lambda m, h: (h, 0)),
            pl.BlockSpec((DK_L, 1), lambda m, h: (h, 0)),
            pl.BlockSpec((DK_L, 1), lambda m, h: (h, 0)),
            pl.BlockSpec((DV_L, 1), lambda m, h: (0, 0)),
        ],
        out_specs=[
            pl.BlockSpec((DV_L, ntok), lambda m, h: (m * H_L + h, 0)),
            pl.BlockSpec((None, None, DK_L * DV_L, n_seq), lambda m, h: (m, h, 0, 0)),
        ],
        out_shape=[jax.ShapeDtypeStruct((2 * V_L, ntok), F32),
                   jax.ShapeDtypeStruct((2, H_L, DK_L * DV_L, n_seq), F32)],
        compiler_params=_params("parallel", "parallel"),
        name="linattn_sample",
    )(zlt, zlt, zlt, zlt, zlt, sg, sr, wgt, bg_col, lgam_col, ggla_col)
    return o, s1


def _out_proj_prompt_kernel(oa_ref, obc_ref, x_ref, w_ref, y_ref):
    y = _nn(oa_ref[...].astype(BF16), w_ref[0:W_A, :]) + _nn(obc_ref[...].astype(BF16), w_ref[W_A:, :])
    y_ref[...] = x_ref[...] + y


def _out_proj_prompt(oa, obc, x, w, tm=512):
    t = x.shape[0]
    half = pl.BlockSpec((tm, W_A), lambda i: (i, 0))
    full = pl.BlockSpec((tm, D_MODEL), lambda i: (i, 0))
    return pl.pallas_call(
        _out_proj_prompt_kernel,
        grid=(t // tm,),
        in_specs=[half, half, full, pl.BlockSpec((D_MODEL, D_MODEL), lambda i: (0, 0))],
        out_specs=full,
        out_shape=jax.ShapeDtypeStruct((t, D_MODEL), F32),
        compiler_params=_params("parallel"),
        name="out_proj_prompt",
    )(oa, obc, x, w)


def _out_proj_sample_kernel(oa_ref, obct_ref, x_ref, w_ref, y_ref):
    y = _nn(oa_ref[...].astype(BF16), w_ref[0:W_A, :]) + _tn(obct_ref[...].astype(BF16), w_ref[W_A:, :])
    y_ref[...] = x_ref[...] + y


def _out_proj_sample(oa, obct, x, w):
    return pl.pallas_call(
        _out_proj_sample_kernel,
        out_shape=jax.ShapeDtypeStruct(x.shape, F32),
        compiler_params=pltpu.CompilerParams(vmem_limit_bytes=VMEM_LIMIT),
        name="out_proj_sample",
    )(oa, obct, x, w)


def _bitonic_sort_desc(x):
    x = list(x)
    n = len(x)
    k = 2
    while k <= n:
        j = k // 2
        while j >= 1:
            for i in range(n):
                l = i ^ j
                if l > i:
                    hi, lo = jnp.maximum(x[i], x[l]), jnp.minimum(x[i], x[l])
                    x[i], x[l] = (hi, lo) if (i & k) == 0 else (lo, hi)
            j //= 2
        k *= 2
    return x


def _merge_top(a, b):
    n = len(a)
    x = [jnp.maximum(a[i], b[n - 1 - i]) for i in range(n)]
    j = n // 2
    while j >= 1:
        for i in range(n):
            l = i ^ j
            if l > i:
                x[i], x[l] = jnp.maximum(x[i], x[l]), jnp.minimum(x[i], x[l])
        j //= 2
    return x


def _top16_rows(s):
    t = s.shape[1]
    slabs = s.reshape(N_KEYS // SUBLANE, SUBLANE, t)
    x = _bitonic_sort_desc([slabs[i] for i in range(N_KEYS // SUBLANE)])
    for shift in (4, 2, 1):
        x = _merge_top(x, [pltpu.roll(xi, shift, 0) for xi in x])
    return x


def _top16_pair_sums(t1, t2):
    k = PEER_TOPK
    cand = [[t1[a] + t2[b] for b in range(k // (a + 1))] for a in range(k)]
    neg = jnp.full_like(t1[0], -jnp.inf)
    g0 = cand[0]
    g1 = _bitonic_sort_desc(cand[1] + cand[2] + cand[3][0:3])
    g2 = _bitonic_sort_desc(cand[3][3:4] + cand[4] + cand[5] + cand[6] + cand[7] + [cand[a][0] for a in range(8, 14)])
    g3 = [jnp.maximum(cand[14][0], cand[15][0]), jnp.minimum(cand[14][0], cand[15][0])] + [neg] * (k - 2)
    return _merge_top(_merge_top(g0, g1), _merge_top(g2, g3)), cand


PEER_GROUP = 8
PEER_E_BLK = 2048


def _peer_kernel(x_ref, g_ref, wpqt_ref, keys_ref, u_ref, vt_ref, gfin_ref, y_ref,
                 hn_scr, rank_scr, f2_scr, nsel_scr, f1_scr, acc_scr, pre_scr, act_scr, top_scr, count_scr, z_scr,
                 *, final_norm):
    e = pl.program_id(1)
    tt = hn_scr.shape[0]
    e_blk = u_ref.shape[0]
    nslab = N_KEYS // SUBLANE
    grp_rows = PEER_GROUP * N_KEYS
    nj = tt // LANE

    @pl.when(e == 0)
    def _route():
        hn_scr[...] = _rms(x_ref[...], g_ref[...]).astype(BF16)
        acc_scr[...] = jnp.zeros_like(acc_scr)

        def scores(h, carry):
            w_rows = pl.ds(pl.multiple_of(h * 2 * N_KEYS, 2 * N_KEYS), 2 * N_KEYS)
            qk = _nt(wpqt_ref[w_rows, :], hn_scr[...])
            for half, dst in ((0, f1_scr), (1, nsel_scr)):
                s = _nn(keys_ref[half], qk[half * N_KEYS:(half + 1) * N_KEYS], precision=HIGHEST)
                dst[h] = s
                for r, t in enumerate(_top16_rows(s)):
                    top_scr[half, r, pl.ds(h, 1), :] = t[0:1, :]
            return carry

        lax.fori_loop(0, PEER_HEADS, scores, 0)

        t1 = [top_scr[0, r] for r in range(PEER_TOPK)]
        t2 = [top_scr[1, r] for r in range(PEER_TOPK)]
        top, cand = _top16_pair_sums(t1, t2)
        tau = top[PEER_TOPK - 1]
        z = jnp.exp(top[0] - top[0])
        for r in range(1, PEER_TOPK):
            z = z + jnp.exp(top[r] - top[0])
        z_scr[...] = z
        for r in range(PEER_TOPK):
            count_scr[r] = sum(jnp.where(c >= tau, 1.0, 0.0) for c in cand[r])

        def gates(h, carry):
            row = lambda ref, *idx: jnp.broadcast_to(ref[idx + (pl.ds(h, 1), slice(None))], (SUBLANE, tt))[None]
            s1s = f1_scr[h].reshape(nslab, SUBLANE, tt)
            s2s = nsel_scr[h].reshape(nslab, SUBLANE, tt)
            nsel = jnp.zeros((nslab, SUBLANE, tt), F32)
            rank = jnp.zeros((nslab, SUBLANE, tt), F32)
            for r in reversed(range(PEER_TOPK)):
                nsel = jnp.where(s1s >= row(top_scr, 0, r), row(count_scr, r), nsel)
            for r in range(PEER_TOPK):
                rank = jnp.where(row(top_scr, 1, r) > s2s, float(r + 1), rank)
            nsel_scr[h] = nsel.reshape(N_KEYS, tt)
            f1_scr[h] = jnp.exp(s1s - row(top_scr, 0, 0)).reshape(N_KEYS, tt)
            rank_b = rank.reshape(N_KEYS, tt).astype(BF16)
            f2_b = (jnp.exp(s2s - row(top_scr, 1, 0)) / row(z_scr)).reshape(N_KEYS, tt).astype(BF16)
            for j in range(nj):
                rank_scr[h, j] = rank_b[:, j * LANE:(j + 1) * LANE]
                f2_scr[h, j] = f2_b[:, j * LANE:(j + 1) * LANE]
            return carry

        lax.fori_loop(0, PEER_HEADS, gates, 0)

    def pre_activations(g):
        rows = slice(g * grp_rows, (g + 1) * grp_rows)
        pre = _nt(u_ref[rows, :], hn_scr[...])
        for j in range(nj):
            pre_scr[g % 2, j] = pre[:, j * LANE:(j + 1) * LANE]

    def gated_activations(g):
        for al in range(PEER_GROUP):
            a = e * (e_blk // N_KEYS) + g * PEER_GROUP + al
            rows = slice(al * N_KEYS, (al + 1) * N_KEYS)
            n_rows = [nsel_scr[h, pl.ds(a, 1), :] for h in range(PEER_HEADS)]
            f1_rows = [f1_scr[h, pl.ds(a, 1), :] for h in range(PEER_HEADS)]
            for j in range(nj):
                ln = slice(j * LANE, (j + 1) * LANE)
                gate = jnp.zeros((N_KEYS, LANE), BF16)
                for h in range(PEER_HEADS):
                    n_row = jnp.broadcast_to(n_rows[h][:, ln], (N_KEYS, LANE)).astype(BF16)
                    f1_row = jnp.broadcast_to(f1_rows[h][:, ln], (N_KEYS, LANE)).astype(BF16)
                    gate = gate + jnp.where(rank_scr[h, j] < n_row, f2_scr[h, j], 0.0) * f1_row
                p = pre_scr[g % 2, j, rows, :]
                gelu = 0.5 * p * (1.0 + lax.erf(p * (2.0 ** -0.5)))
                act_scr[g % 2, j, rows, :] = gelu.astype(BF16) * gate

    def accumulate(g):
        cols = slice(g * grp_rows, (g + 1) * grp_rows)
        act = jnp.concatenate([act_scr[g % 2, j] for j in range(nj)], axis=1)
        acc_scr[...] += _nn(vt_ref[:, cols], act)

    n_grp = e_blk // grp_rows
    pre_activations(0)
    for g in range(n_grp):
        if g + 1 < n_grp:
            pre_activations(g + 1)
        gated_activations(g)
        if g >= 1:
            accumulate(g - 1)
    accumulate(n_grp - 1)

    @pl.when(e == pl.num_programs(1) - 1)
    def _finish():
        y = x_ref[...] + acc_scr[...].T
        if final_norm:
            y = _rms(y, gfin_ref[...])
        y_ref[...] = y


def _peer(x, g, wpqt, keys, u, vtb, layer, gfin, final_norm, tt):
    t = x.shape[0]
    e_blk = PEER_E_BLK
    once = dict(pipeline_mode=pl.Buffered(1))
    return pl.pallas_call(
        functools.partial(_peer_kernel, final_norm=final_norm),
        grid=(t // tt, N_EXPERTS // e_blk),
        in_specs=[
            pl.BlockSpec((tt, D_MODEL), lambda i, e: (i, 0), **once),
            pl.BlockSpec((1, D_MODEL), lambda i, e: (0, 0)),
            pl.BlockSpec((2 * PEER_HEADS * N_KEYS, D_MODEL), lambda i, e: (0, 0), **once),
            pl.BlockSpec((2, N_KEYS, N_KEYS), lambda i, e: (0, 0, 0)),
            pl.BlockSpec((None, e_blk, D_MODEL), lambda i, e: (layer, e, 0)),
            pl.BlockSpec((None, None, D_MODEL, e_blk), lambda i, e: (layer, e, 0, 0)),
            pl.BlockSpec((1, D_MODEL), lambda i, e: (0, 0)),
        ],
        out_specs=pl.BlockSpec((tt, D_MODEL), lambda i, e: (i, 0), **once),
        out_shape=jax.ShapeDtypeStruct((t, D_MODEL), F32),
        scratch_shapes=[
            pltpu.VMEM((tt, D_MODEL), BF16),
            pltpu.VMEM((PEER_HEADS, tt // LANE, N_KEYS, LANE), BF16),
            pltpu.VMEM((PEER_HEADS, tt // LANE, N_KEYS, LANE), BF16),
            pltpu.VMEM((PEER_HEADS, N_KEYS, tt), F32),
            pltpu.VMEM((PEER_HEADS, N_KEYS, tt), F32),
            pltpu.VMEM((D_MODEL, tt), F32),
            pltpu.VMEM((2, tt // LANE, PEER_GROUP * N_KEYS, LANE), F32),
            pltpu.VMEM((2, tt // LANE, PEER_GROUP * N_KEYS, LANE), BF16),
            pltpu.VMEM((2, PEER_TOPK, PEER_HEADS, tt), F32),
            pltpu.VMEM((PEER_TOPK, PEER_HEADS, tt), F32),
            pltpu.VMEM((PEER_HEADS, tt), F32),
        ],
        compiler_params=_params("parallel", "arbitrary", vmem=VMEM_PHYSICAL * 15 // 16),
        name="peer",
    )(x, g, wpqt, keys, u, vtb, gfin)


def _split_w_in(w):
    sizes = (W_A, W_A, W_A, QK_L, QK_L, V_L, V_L, GATE_RANK, QK_L, QK_L, V_L, V_L)
    out, start = [], 0
    for n in sizes:
        out.append(w[:, start:start + n])
        start += n
    return out


def _layer_weights(w_in, w_gate2, b_gate, g_gla, w_out, w_pq):
    qa, ka, va, qb, kb, vb, rb, gb, qc, kc, vc, gc = _split_w_in(w_in)
    gb = jnp.pad(gb, ((0, 0), (0, LANE - GATE_RANK)))
    wa = jnp.concatenate([qa, ka, va], axis=1).astype(BF16)
    wl = jnp.concatenate([qb, qc, kb, kc, gb, vb, vc, rb, gc], axis=1).astype(BF16)
    wg = jnp.pad(w_gate2, ((0, LANE - GATE_RANK), (0, 0)))
    log_gamma = jnp.log(1.0 - 2.0 ** (-5.0 - jnp.arange(H_L, dtype=F32)))
    lgam = jnp.repeat(log_gamma, DK_L)
    return dict(
        wa=wa, wl=wl, wlt=wl.T, wg=wg, wgt=wg.T,
        wkvt=jnp.concatenate([ka, va], axis=1).T.astype(BF16),
        bg=b_gate.reshape(1, QK_L), bg_col=b_gate.reshape(QK_L, 1),
        lgam=lgam.reshape(1, QK_L), lgam_col=lgam.reshape(QK_L, 1),
        ggla=jnp.tile(g_gla, H_L).reshape(1, V_L), ggla_col=g_gla.reshape(DV_L, 1),
        w_out=w_out.astype(BF16), wpqt=w_pq.T.astype(BF16),
    )


def kernel(x_prompt, x_sample, cache_kv_win, state_gla, state_ret, w_in, w_gate2, b_gate, g_gla,
           w_out, g_mix, g_ffn, w_pq, sub_keys, u_tab, v_tab, g_final):
    batch, seq, _ = x_prompt.shape
    n_seq, t_new, _ = x_sample.shape
    depth = w_in.shape[0]
    win = min(MAX_WINDOW, seq)
    xp = x_prompt.reshape(batch * seq, D_MODEL)
    xs = x_sample.transpose(1, 0, 2).reshape(t_new * n_seq, D_MODEL)
    tables = _sample_attn_tables(t_new)
    gfin = g_final.reshape(1, D_MODEL)
    cache_t = cache_kv_win.transpose(0, 1, 3, 4, 5, 2)
    sg = state_gla.transpose(0, 2, 3, 4, 1).reshape(depth, H_L, DK_L * DV_L, n_seq)
    sr = state_ret.transpose(0, 2, 3, 4, 1).reshape(depth, H_L, DK_L * DV_L, n_seq)
    u_all = u_tab.astype(BF16)
    vtb_all = (v_tab.reshape(depth, N_EXPERTS // PEER_E_BLK, PEER_E_BLK, D_MODEL)
               .transpose(0, 1, 3, 2).astype(BF16))
    kv_p, kv_s, gla_p, gla_s, ret_p, ret_s = [], [], [], [], [], []
    for l in range(depth):
        w = _layer_weights(w_in[l], w_gate2[l], b_gate[l], g_gla[l], w_out[l], w_pq[l])
        gm = g_mix[l].reshape(1, D_MODEL)
        gf = g_ffn[l].reshape(1, D_MODEL)
        last = l == depth - 1

        za, zl = _in_proj(xp, gm, w["wa"], w["wl"])
        kv_p.append(_kv_tail(xp, gm, w["wkvt"], batch, seq, win).reshape(batch, 2, H_A, HEAD_DIM, win))
        oa = _attn_prompt(za, batch, seq)
        obc, sfin = _linattn_prompt(zl, w["wg"], w["bg"], w["lgam"], w["ggla"], batch, seq)
        xp = _out_proj_prompt(oa, obc, xp, w["w_out"])
        xp = _peer(xp, gf, w["wpqt"], sub_keys[l], u_all, vtb_all, l, gfin, last, tt=512)
        gla_p.append(sfin[:, 0])
        ret_p.append(sfin[:, 1])

        za_s, zlt = _in_proj_sample(xs, gm, w["wa"], w["wlt"])
        oa_s = _attn_sample(za_s, cache_t, l, tables, n_seq, t_new)
        obct, s1 = _linattn_sample(zlt, sg, sr, l, w["wgt"], w["bg_col"], w["lgam_col"], w["ggla_col"],
                                   n_seq, t_new)
        xs = _out_proj_sample(oa_s, obct, xs, w["w_out"])
        xs = _peer(xs, gf, w["wpqt"], sub_keys[l], u_all, vtb_all, l, gfin, last, tt=512)
        kv_new = za_s.reshape(t_new, n_seq, 3, H_A, HEAD_DIM)[:, :, 1:3].transpose(1, 0, 2, 3, 4)
        kv_s.append(kv_new)
        gla_s.append(s1[0])
        ret_s.append(s1[1])

    y_prompt = xp.reshape(batch, seq, D_MODEL)
    y_sample = xs.reshape(t_new, n_seq, D_MODEL).transpose(1, 0, 2)
    seq_major = lambda s: jnp.stack(s).reshape(depth, H_L, DK_L, DV_L, n_seq).transpose(0, 4, 1, 2, 3)
    kv_prompt = jnp.stack(kv_p).transpose(0, 1, 5, 2, 3, 4)
    return (y_prompt, y_sample, kv_prompt, jnp.stack(kv_s), jnp.stack(gla_p), seq_major(gla_s),
            jnp.stack(ret_p), seq_major(ret_s))
```

```python
import functools
import math

import numpy as np
import jax
import jax.numpy as jnp
from jax import lax
from jax.experimental import pallas as pl
from jax.experimental.pallas import tpu as pltpu

F32 = jnp.float32
BF16 = jnp.bfloat16
HIGHEST = lax.Precision.HIGHEST

D_MODEL = 1024
HEAD_DIM = 64
H_A = 8
W_A = H_A * HEAD_DIM
BRANCHES = ((128, 1), (512, 4), (2048, 16))
BAND = 128
MAX_WINDOW = 2048
H_L = 4
DK_L = 32
DV_L = 64
QK_L = H_L * DK_L
V_L = H_L * DV_L
GATE_RANK = 16
GATE_TAU = 16.0
N_KEYS = 128
N_EXPERTS = N_KEYS * N_KEYS
PEER_HEADS = 8
PEER_TOPK = 16
NORM_EPS = 1e-6

LANE = 128
SUBLANE = 8
W_LIN = 2 * 2 * QK_L + LANE + 2 * V_L + 2 * V_L
VMEM_PHYSICAL = 64 * 1024 * 1024
VMEM_LIMIT = VMEM_PHYSICAL * 7 // 8


def _params(*sem, vmem=VMEM_LIMIT):
    return pltpu.CompilerParams(dimension_semantics=sem, vmem_limit_bytes=vmem)


def _rms(x, g):
    return x * lax.rsqrt(jnp.mean(x * x, axis=-1, keepdims=True) + NORM_EPS) * g


def _nt(a, b, **kw):
    return lax.dot_general(a, b, (((1,), (1,)), ((), ())), preferred_element_type=F32, **kw)


def _tn(a, b, **kw):
    return lax.dot_general(a, b, (((0,), (0,)), ((), ())), preferred_element_type=F32, **kw)


def _nn(a, b, **kw):
    return jnp.dot(a, b, preferred_element_type=F32, **kw)


def _in_proj_kernel(x_ref, g_ref, wa_ref, wl_ref, za_ref, zl_ref):
    h = _rms(x_ref[...], g_ref[...]).astype(BF16)
    za_ref[...] = _nn(h, wa_ref[...])
    zl_ref[...] = _nn(h, wl_ref[...])


def _in_proj(x, g, wa, wl, tm=256):
    t = x.shape[0]
    return pl.pallas_call(
        _in_proj_kernel,
        grid=(t // tm,),
        in_specs=[
            pl.BlockSpec((tm, D_MODEL), lambda i: (i, 0)),
            pl.BlockSpec((1, D_MODEL), lambda i: (0, 0)),
            pl.BlockSpec((D_MODEL, 3 * W_A), lambda i: (0, 0)),
            pl.BlockSpec((D_MODEL, W_LIN), lambda i: (0, 0)),
        ],
        out_specs=[
            pl.BlockSpec((tm, 3 * W_A), lambda i: (i, 0)),
            pl.BlockSpec((tm, W_LIN), lambda i: (i, 0)),
        ],
        out_shape=[jax.ShapeDtypeStruct((t, 3 * W_A), F32), jax.ShapeDtypeStruct((t, W_LIN), F32)],
        compiler_params=_params("parallel"),
        name="in_proj",
    )(x, g, wa, wl)


def _kv_tail_kernel(x_ref, g_ref, wkvt_ref, o_ref):
    h = _rms(x_ref[...], g_ref[...]).astype(BF16)
    o_ref[...] = _nt(wkvt_ref[...], h)


def _kv_tail(x, g, wkvt, batch, seq, win, tm=512):
    per_seq, first = seq // tm, (seq - win) // tm
    return pl.pallas_call(
        _kv_tail_kernel,
        grid=(batch, win // tm),
        in_specs=[
            pl.BlockSpec((tm, D_MODEL), lambda b, i: (b * per_seq + first + i, 0)),
            pl.BlockSpec((1, D_MODEL), lambda b, i: (0, 0)),
            pl.BlockSpec((2 * W_A, D_MODEL), lambda b, i: (0, 0)),
        ],
        out_specs=pl.BlockSpec((None, 2 * W_A, tm), lambda b, i: (b, 0, i)),
        out_shape=jax.ShapeDtypeStruct((batch, 2 * W_A, win), F32),
        compiler_params=_params("parallel", "parallel"),
        name="kv_tail",
    )(x, g, wkvt)


def _in_proj_sample_kernel(x_ref, g_ref, wa_ref, wlt_ref, za_ref, zlt_ref):
    h = _rms(x_ref[...], g_ref[...]).astype(BF16)
    za_ref[...] = _nn(h, wa_ref[...])
    zlt_ref[...] = _nt(wlt_ref[...], h)


def _in_proj_sample(x, g, wa, wlt):
    t = x.shape[0]
    return pl.pallas_call(
        _in_proj_sample_kernel,
        out_shape=[jax.ShapeDtypeStruct((t, 3 * W_A), F32), jax.ShapeDtypeStruct((W_LIN, t), F32)],
        compiler_params=pltpu.CompilerParams(vmem_limit_bytes=VMEM_LIMIT),
        name="in_proj_sample",
    )(x, g, wa, wlt)


def _alibi_slope(h):
    return 2.0 ** (-8.0 * (h + 1) / H_A)


ATT_TILE = BAND * max(d for _, d in BRANCHES)


def _attn_prompt_kernel(slope_ref, q_ref, kc_ref, kp_ref, vc_ref, vp_ref, o_ref,
                        k_scr, v_scr, m_scr, l_scr, acc_scr):
    tile = pl.program_id(2)
    k_scr[0:ATT_TILE] = kp_ref[...]
    k_scr[ATT_TILE:2 * ATT_TILE] = kc_ref[...]
    v_scr[0:ATT_TILE] = vp_ref[...]
    v_scr[ATT_TILE:2 * ATT_TILE] = vc_ref[...]
    qi = lax.broadcasted_iota(jnp.int32, (BAND, 2 * BAND), 0)
    ki = lax.broadcasted_iota(jnp.int32, (BAND, 2 * BAND), 1)
    dist = qi + BAND - ki
    in_band = (dist >= 0) & (dist <= BAND)
    distf = dist.astype(F32)
    low = lax.broadcasted_iota(jnp.int32, (BAND, LANE), 1) < HEAD_DIM
    slopes = [slope_ref[sub][0:1, 0:1] for sub in range(2)]

    def unit(u, carry, *, d, first):
        res, blk = u % d, u // d
        start = res + d * BAND * blk
        stride = None if d == 1 else d
        rows = pl.ds(start, BAND, stride=stride)
        keys = pl.ds(ATT_TILE + start - d * BAND, 2 * BAND, stride=stride)
        q = q_ref[rows, :]
        k = k_scr[keys, :].astype(BF16)
        v = v_scr[keys, :].astype(BF16)
        first_key = jnp.where((tile == 0) & (blk == 0), BAND, 0)
        valid = in_band & (ki >= first_key)
        ms, ls, pvs = [], [], []
        for sub in range(2):
            qm = jnp.where(low if sub == 0 else jnp.logical_not(low), q, 0.0).astype(BF16)
            s = _nt(qm, k) * (HEAD_DIM ** -0.5)
            s = jnp.where(valid, s - (slopes[sub] * float(d)) * distf, -jnp.inf)
            m = jnp.max(s, axis=-1, keepdims=True)
            e = jnp.exp(s - m)
            ms.append(m)
            ls.append(jnp.sum(e, axis=-1, keepdims=True))
            pvs.append(_nn(e.astype(BF16), v))
        m_u = jnp.where(low, ms[0], ms[1])
        l_u = jnp.where(low, ls[0], ls[1])
        acc_u = jnp.where(low, pvs[0], pvs[1])
        if first:
            m_scr[rows, :] = m_u
            l_scr[rows, :] = l_u
            acc_scr[rows, :] = acc_u
        else:
            m_o = m_scr[rows, :]
            m_n = jnp.maximum(m_o, m_u)
            w_o, w_u = jnp.exp(m_o - m_n), jnp.exp(m_u - m_n)
            m_scr[rows, :] = m_n
            l_scr[rows, :] = w_o * l_scr[rows, :] + w_u * l_u
            acc_scr[rows, :] = w_o * acc_scr[rows, :] + w_u * acc_u
        return carry

    for n, (_, d) in enumerate(BRANCHES):
        lax.fori_loop(0, ATT_TILE // BAND, functools.partial(unit, d=d, first=n == 0), 0, unroll=16)
    o_ref[...] = acc_scr[...] / l_scr[...]


def _attn_prompt(za, batch, seq):
    ntile = seq // ATT_TILE
    npair = H_A // 2
    slope_tab = jnp.asarray(np.broadcast_to(
        np.array([_alibi_slope(h) for h in range(H_A)], np.float32).reshape(npair, 2, 1, 1),
        (npair, 2, SUBLANE, LANE)))
    blk = (ATT_TILE, LANE)
    prev = lambda t: jnp.maximum(t - 1, 0)
    return pl.pallas_call(
        _attn_prompt_kernel,
        grid=(batch, npair, ntile),
        in_specs=[
            pl.BlockSpec((None, 2, SUBLANE, LANE), lambda b, p, t: (p, 0, 0, 0)),
            pl.BlockSpec(blk, lambda b, p, t: (b * ntile + t, p)),
            pl.BlockSpec(blk, lambda b, p, t: (b * ntile + t, npair + p)),
            pl.BlockSpec(blk, lambda b, p, t: (b * ntile + prev(t), npair + p)),
            pl.BlockSpec(blk, lambda b, p, t: (b * ntile + t, 2 * npair + p)),
            pl.BlockSpec(blk, lambda b, p, t: (b * ntile + prev(t), 2 * npair + p)),
        ],
        out_specs=pl.BlockSpec(blk, lambda b, p, t: (b * ntile + t, p)),
        out_shape=jax.ShapeDtypeStruct((batch * seq, W_A), F32),
        scratch_shapes=[pltpu.VMEM((2 * ATT_TILE, LANE), F32), pltpu.VMEM((2 * ATT_TILE, LANE), F32),
                        pltpu.VMEM((ATT_TILE, LANE), F32), pltpu.VMEM((ATT_TILE, LANE), F32),
                        pltpu.VMEM((ATT_TILE, LANE), F32)],
        compiler_params=_params("parallel", "parallel", "arbitrary"),
        name="attn_prompt",
    )(slope_tab, za, za, za, za, za)


def _sample_attn_tables(t_new):
    n_past = MAX_WINDOW

    def entry(h, t, idx):
        delta = n_past + t - idx
        if delta < 0:
            return -np.inf
        cnt = sum(1 for (w, d) in BRANCHES if delta % d == 0 and delta <= w)
        return -_alibi_slope(h) * delta + math.log(cnt) if cnt else -np.inf

    past = np.zeros((H_A, SUBLANE, MAX_WINDOW), np.float32)
    new = np.zeros((H_A, SUBLANE, SUBLANE), np.float32)
    for h in range(H_A):
        for t in range(t_new):
            past[h, t] = [entry(h, t, idx) for idx in range(MAX_WINDOW)]
            new[h, t] = [entry(h, t, n_past + s) if s < t_new else -np.inf for s in range(SUBLANE)]
    return jnp.asarray(past), jnp.asarray(new)


def _attn_sample_kernel(za_ref, kt_ref, vt_ref, bp_ref, bn_ref, o_ref, *, t_new):
    zero_rows = jnp.zeros((SUBLANE - t_new, W_A), F32)
    rows = [za_ref[t] for t in range(t_new)]
    q8 = jnp.concatenate([r[:, 0:W_A] for r in rows] + [zero_rows], axis=0)
    k8 = jnp.concatenate([r[:, W_A:2 * W_A] for r in rows] + [zero_rows], axis=0)
    v8 = jnp.concatenate([r[:, 2 * W_A:3 * W_A] for r in rows] + [zero_rows], axis=0)
    scale = HEAD_DIM ** -0.5
    outs = []
    for h in range(H_A):
        sl = slice(h * HEAD_DIM, (h + 1) * HEAD_DIM)
        qh = q8[:, sl].astype(BF16)
        s_past = _nn(qh, kt_ref[h].astype(BF16)) * scale + bp_ref[h]
        s_new = _nt(qh, k8[:, sl].astype(BF16)) * scale + bn_ref[h]
        m = jnp.maximum(s_past.max(axis=-1, keepdims=True), s_new.max(axis=-1, keepdims=True))
        p_past = jnp.exp(s_past - m)
        p_new = jnp.exp(s_new - m)
        den = p_past.sum(axis=-1, keepdims=True) + p_new.sum(axis=-1, keepdims=True)
        o = _nt(p_past.astype(BF16), vt_ref[h].astype(BF16)) + _nn(p_new.astype(BF16), v8[:, sl].astype(BF16))
        outs.append(o / den)
    out = jnp.concatenate(outs, axis=-1)
    for t in range(t_new):
        o_ref[t] = out[t:t + 1, :]


def _attn_sample(za, cache_t, layer, tables, n_seq, t_new):
    bp, bn = tables
    zav = za.reshape(t_new, n_seq, 1, 3 * W_A)
    window = lambda kv: pl.BlockSpec((None, None, None, H_A, HEAD_DIM, MAX_WINDOW),
                                     lambda b: (layer, b, kv, 0, 0, 0))
    o = pl.pallas_call(
        functools.partial(_attn_sample_kernel, t_new=t_new),
        grid=(n_seq,),
        in_specs=[
            pl.BlockSpec((t_new, None, 1, 3 * W_A), lambda b: (0, b, 0, 0)),
            window(0), window(1),
            pl.BlockSpec(bp.shape, lambda b: (0, 0, 0)),
            pl.BlockSpec(bn.shape, lambda b: (0, 0, 0)),
        ],
        out_specs=pl.BlockSpec((t_new, None, 1, W_A), lambda b: (0, b, 0, 0)),
        out_shape=jax.ShapeDtypeStruct((t_new, n_seq, 1, W_A), F32),
        compiler_params=_params("parallel"),
        name="attn_sample",
    )(zav, cache_t, cache_t, bp, bn)
    return o.reshape(t_new * n_seq, W_A)


CHUNK = 128


def _linattn_prompt_kernel(zl_ref, wg_ref, bg_ref, lgam_ref, ggla_ref, obc_ref, sfin_ref, s_scr):
    j = pl.program_id(1)

    @pl.when(j == 0)
    def _():
        s_scr[...] = jnp.zeros_like(s_scr)

    c = CHUNK
    z = zl_ref[...]
    q = z[:, 0:2 * QK_L] * (DK_L ** -0.5)
    k = z[:, 2 * QK_L:4 * QK_L]
    gb = z[:, 4 * QK_L:4 * QK_L + LANE]
    v = z[:, 4 * QK_L + LANE:4 * QK_L + LANE + 2 * V_L]
    gates = z[:, 4 * QK_L + LANE + 2 * V_L:]

    pre = _nn(gb, wg_ref[...], precision=HIGHEST) + bg_ref[...]
    la = jnp.concatenate([jax.nn.log_sigmoid(pre) / GATE_TAU, jnp.broadcast_to(lgam_ref[...], (c, QK_L))], axis=1)
    row = lax.broadcasted_iota(jnp.int32, (c, c), 0)
    col = lax.broadcasted_iota(jnp.int32, (c, c), 1)
    causal = col <= row
    b = _nn(causal.astype(F32), la, precision=HIGHEST)
    mid = b[c // 2 - 1:c // 2, :]
    last = b[c - 1:c, :]
    qt = q * jnp.exp(b - mid)
    kt = k * jnp.exp(mid - b)
    qi = (q * jnp.exp(b)).astype(BF16)
    kh = (k * jnp.exp(last - b)).astype(BF16)
    ones = jnp.ones((c, LANE), F32)

    head_qk = lax.broadcasted_iota(jnp.int32, (c, QK_L), 1) // DK_L
    head_v = lax.broadcasted_iota(jnp.int32, (c, V_L), 1) // DV_L
    blockdiag = (lax.broadcasted_iota(jnp.int32, (QK_L, V_L), 0) // DK_L
                 == lax.broadcasted_iota(jnp.int32, (QK_L, V_L), 1) // DV_L)
    causal4 = (lax.broadcasted_iota(jnp.int32, (H_L * c, c), 1)
               <= lax.broadcasted_iota(jnp.int32, (H_L * c, c), 0) % c)
    outs = []
    for mix in range(2):
        sl = slice(mix * QK_L, (mix + 1) * QK_L)
        vm = v[:, mix * V_L:(mix + 1) * V_L].astype(BF16)
        qstack = jnp.concatenate([jnp.where(head_qk == h, qt[:, sl], 0.0) for h in range(H_L)], axis=0)
        att = _nt(qstack.astype(BF16), kt[:, sl].astype(BF16))
        att = jnp.where(causal4, att, 0.0).astype(BF16)
        r = _nn(att, vm)
        o = jnp.zeros((c, V_L), F32)
        for h in range(H_L):
            o = o + jnp.where(head_v == h, r[h * c:(h + 1) * c, :], 0.0)
        s = s_scr[mix]
        o = o + _nn(qi[:, sl], s.astype(BF16))
        decay = jnp.exp(_tn(la[:, sl], ones, precision=HIGHEST))
        kv = _tn(kh[:, sl], vm)
        s_scr[mix] = jnp.concatenate([decay, decay], axis=1) * s + jnp.where(blockdiag, kv, 0.0)
        outs.append(o)

    seg = (lax.broadcasted_iota(jnp.int32, (V_L, V_L), 0) // DV_L
           == lax.broadcasted_iota(jnp.int32, (V_L, V_L), 1) // DV_L).astype(F32) * (1.0 / DV_L)
    ob = outs[0]
    ob = ob * lax.rsqrt(_nn(ob * ob, seg, precision=HIGHEST) + NORM_EPS) * ggla_ref[...]
    ob = ob * jax.nn.silu(gates[:, 0:V_L])
    oc = outs[1]
    dev = oc - _nn(oc, seg, precision=HIGHEST)
    oc = dev * lax.rsqrt(_nn(dev * dev, seg, precision=HIGHEST) + NORM_EPS) * jax.nn.silu(gates[:, V_L:2 * V_L])
    obc_ref[...] = jnp.concatenate([ob, oc], axis=1)

    @pl.when(j == pl.num_programs(1) - 1)
    def _():
        sfin_ref[...] = s_scr[...]


def _linattn_prompt(zl, wg, bg, lgam, ggla, batch, seq):
    nchunk = seq // CHUNK
    obc, sfin = pl.pallas_call(
        _linattn_prompt_kernel,
        grid=(batch, nchunk),
        in_specs=[
            pl.BlockSpec((CHUNK, W_LIN), lambda b, j: (b * nchunk + j, 0)),
            pl.BlockSpec((LANE, QK_L), lambda b, j: (0, 0)),
            pl.BlockSpec((1, QK_L), lambda b, j: (0, 0)),
            pl.BlockSpec((1, QK_L), lambda b, j: (0, 0)),
            pl.BlockSpec((1, V_L), lambda b, j: (0, 0)),
        ],
        out_specs=[
            pl.BlockSpec((CHUNK, 2 * V_L), lambda b, j: (b * nchunk + j, 0)),
            pl.BlockSpec((None, 2, QK_L, V_L), lambda b, j: (b, 0, 0, 0)),
        ],
        out_shape=[jax.ShapeDtypeStruct((batch * seq, 2 * V_L), F32),
                   jax.ShapeDtypeStruct((batch, 2, QK_L, V_L), F32)],
        scratch_shapes=[pltpu.VMEM((2, QK_L, V_L), F32)],
        compiler_params=_params("parallel", "arbitrary"),
        name="linattn_prompt",
    )(zl, wg, bg, lgam, ggla)
    sfin = sfin.reshape(batch, 2, H_L, DK_L, H_L, DV_L)
    return obc, jnp.stack([sfin[:, :, h, :, h, :] for h in range(H_L)], axis=2)


def _linattn_sample_kernel(q_ref, k_ref, gb_ref, v_ref, gate_ref, sg_ref, sr_ref, wgt_ref, bg_ref, lgam_ref,
                           ggla_ref, o_ref, s1_ref, *, n_seq, t_new):
    mix = pl.program_id(0)
    pre = _nn(wgt_ref[...], gb_ref[...], precision=HIGHEST) + bg_ref[...]
    la_gla = jax.nn.log_sigmoid(pre) / GATE_TAU
    la = jnp.where(mix == 0, la_gla, jnp.broadcast_to(lgam_ref[...], la_gla.shape))
    a = jnp.exp(la)
    q = q_ref[...] * (DK_L ** -0.5)
    k = k_ref[...]
    v = v_ref[...]
    s0 = jnp.where(mix == 0, sg_ref[...], sr_ref[...])
    s = [s0[d * DV_L:(d + 1) * DV_L, :] for d in range(DK_L)]
    outs = []
    for t in range(t_new):
        tok = slice(t * n_seq, (t + 1) * n_seq)
        vt = v[:, tok]
        ot = jnp.zeros((DV_L, n_seq), F32)
        for d in range(DK_L):
            s[d] = a[d:d + 1, tok] * s[d] + k[d:d + 1, tok] * vt
            ot = ot + q[d:d + 1, tok] * s[d]
        outs.append(ot)
    s1_ref[...] = jnp.concatenate(s, axis=0)
    o = jnp.concatenate(outs, axis=1)
    mean_sq = jnp.mean(o * o, axis=0, keepdims=True)
    o_gla = o * lax.rsqrt(mean_sq + NORM_EPS) * ggla_ref[...]
    dev = o - jnp.mean(o, axis=0, keepdims=True)
    o_ret = dev * lax.rsqrt(jnp.mean(dev * dev, axis=0, keepdims=True) + NORM_EPS)
    o_ref[...] = jnp.where(mix == 0, o_gla, o_ret) * jax.nn.silu(gate_ref[...])


def _linattn_sample(zlt, sg, sr, layer, wgt, bg_col, lgam_col, ggla_col, n_seq, t_new):
    ntok = t_new * n_seq
    state = pl.BlockSpec((None, None, DK_L * DV_L, n_seq), lambda m, h: (layer, h, 0, 0))
    q0, k0, g0, v0, r0 = 0, 2 * QK_L, 4 * QK_L, 4 * QK_L + LANE, 4 * QK_L + LANE + 2 * V_L
    o, s1 = pl.pallas_call(
        functools.partial(_linattn_sample_kernel, n_seq=n_seq, t_new=t_new),
        grid=(2, H_L),
        in_specs=[
            pl.BlockSpec((DK_L, ntok), lambda m, h: (q0 // DK_L + m * H_L + h, 0)),
            pl.BlockSpec((DK_L, ntok), lambda m, h: (k0 // DK_L + m * H_L + h, 0)),
            pl.BlockSpec((LANE, ntok), lambda m, h: (g0 // LANE, 0)),
            pl.BlockSpec((DV_L, ntok), lambda m, h: (v0 // DV_L + m * H_L + h, 0)),
            pl.BlockSpec((DV_L, ntok), lambda m, h: (r0 // DV_L + m * H_L + h, 0)),
            state, state,
            pl.BlockSpec((DK_L, LANE), lambda m, h: (h, 0)),
            pl.BlockSpec((DK_L, 1), lambda m, h: (h, 0)),
            pl.BlockSpec((DK_L, 1), lambda m, h: (h, 0)),
            pl.BlockSpec((DV_L, 1), lambda m, h: (0, 0)),
        ],
        out_specs=[
            pl.BlockSpec((DV_L, ntok), lambda m, h: (m * H_L + h, 0)),
            pl.BlockSpec((None, None, DK_L * DV_L, n_seq), lambda m, h: (m, h, 0, 0)),
        ],
        out_shape=[jax.ShapeDtypeStruct((2 * V_L, ntok), F32),
                   jax.ShapeDtypeStruct((2, H_L, DK_L * DV_L, n_seq), F32)],
        compiler_params=_params("parallel", "parallel"),
        name="linattn_sample",
    )(zlt, zlt, zlt, zlt, zlt, sg, sr, wgt, bg_col, lgam_col, ggla_col)
    return o, s1


def _out_proj_prompt_kernel(oa_ref, obc_ref, x_ref, w_ref, y_ref):
    y = _nn(oa_ref[...].astype(BF16), w_ref[0:W_A, :]) + _nn(obc_ref[...].astype(BF16), w_ref[W_A:, :])
    y_ref[...] = x_ref[...] + y


def _out_proj_prompt(oa, obc, x, w, tm=512):
    t = x.shape[0]
    half = pl.BlockSpec((tm, W_A), lambda i: (i, 0))
    full = pl.BlockSpec((tm, D_MODEL), lambda i: (i, 0))
    return pl.pallas_call(
        _out_proj_prompt_kernel,
        grid=(t // tm,),
        in_specs=[half, half, full, pl.BlockSpec((D_MODEL, D_MODEL), lambda i: (0, 0))],
        out_specs=full,
        out_shape=jax.ShapeDtypeStruct((t, D_MODEL), F32),
        compiler_params=_params("parallel"),
        name="out_proj_prompt",
    )(oa, obc, x, w)


def _out_proj_sample_kernel(oa_ref, obct_ref, x_ref, w_ref, y_ref):
    y = _nn(oa_ref[...].astype(BF16), w_ref[0:W_A, :]) + _tn(obct_ref[...].astype(BF16), w_ref[W_A:, :])
    y_ref[...] = x_ref[...] + y


def _out_proj_sample(oa, obct, x, w):
    return pl.pallas_call(
        _out_proj_sample_kernel,
        out_shape=jax.ShapeDtypeStruct(x.shape, F32),
        compiler_params=pltpu.CompilerParams(vmem_limit_bytes=VMEM_LIMIT),
        name="out_proj_sample",
    )(oa, obct, x, w)


def _bitonic_sort_desc(x):
    x = list(x)
    n = len(x)
    k = 2
    while k <= n:
        j = k // 2
        while j >= 1:
            for i in range(n):
                l = i ^ j
                if l > i:
                    hi, lo = jnp.maximum(x[i], x[l]), jnp.minimum(x[i], x[l])
                    x[i], x[l] = (hi, lo) if (i & k) == 0 else (lo, hi)
            j //= 2
        k *= 2
    return x


def _merge_top(a, b):
    n = len(a)
    x = [jnp.maximum(a[i], b[n - 1 - i]) for i in range(n)]
    j = n // 2
    while j >= 1:
        for i in range(n):
            l = i ^ j
            if l > i:
                x[i], x[l] = jnp.maximum(x[i], x[l]), jnp.minimum(x[i], x[l])
        j //= 2
    return x


def _top16_rows(s):
    t = s.shape[1]
    slabs = s.reshape(N_KEYS // SUBLANE, SUBLANE, t)
    x = _bitonic_sort_desc([slabs[i] for i in range(N_KEYS // SUBLANE)])
    for shift in (4, 2, 1):
        x = _merge_top(x, [pltpu.roll(xi, shift, 0) for xi in x])
    return x


def _top16_pair_sums(t1, t2):
    k = PEER_TOPK
    cand = [[t1[a] + t2[b] for b in range(k // (a + 1))] for a in range(k)]
    neg = jnp.full_like(t1[0], -jnp.inf)
    g0 = cand[0]
    g1 = _bitonic_sort_desc(cand[1] + cand[2] + cand[3][0:3])
    g2 = _bitonic_sort_desc(cand[3][3:4] + cand[4] + cand[5] + cand[6] + cand[7] + [cand[a][0] for a in range(8, 14)])
    g3 = [jnp.maximum(cand[14][0], cand[15][0]), jnp.minimum(cand[14][0], cand[15][0])] + [neg] * (k - 2)
    return _merge_top(_merge_top(g0, g1), _merge_top(g2, g3)), cand


PEER_GROUP = 8
PEER_E_BLK = 2048


def _peer_kernel(x_ref, g_ref, wpqt_ref, keys_ref, u_ref, vt_ref, gfin_ref, y_ref,
                 hn_scr, rank_scr, f2_scr, nsel_scr, f1_scr, acc_scr, pre_scr, act_scr, top_scr, count_scr, z_scr,
                 *, final_norm):
    e = pl.program_id(1)
    tt = hn_scr.shape[0]
    e_blk = u_ref.shape[0]
    nslab = N_KEYS // SUBLANE
    grp_rows = PEER_GROUP * N_KEYS
    nj = tt // LANE

    @pl.when(e == 0)
    def _route():
        hn_scr[...] = _rms(x_ref[...], g_ref[...]).astype(BF16)
        acc_scr[...] = jnp.zeros_like(acc_scr)

        def scores(h, carry):
            w_rows = pl.ds(pl.multiple_of(h * 2 * N_KEYS, 2 * N_KEYS), 2 * N_KEYS)
            qk = _nt(wpqt_ref[w_rows, :], hn_scr[...])
            for half, dst in ((0, f1_scr), (1, nsel_scr)):
                s = _nn(keys_ref[half], qk[half * N_KEYS:(half + 1) * N_KEYS], precision=HIGHEST)
                dst[h] = s
                for r, t in enumerate(_top16_rows(s)):
                    top_scr[half, r, pl.ds(h, 1), :] = t[0:1, :]
            return carry

        lax.fori_loop(0, PEER_HEADS, scores, 0)

        t1 = [top_scr[0, r] for r in range(PEER_TOPK)]
        t2 = [top_scr[1, r] for r in range(PEER_TOPK)]
        top, cand = _top16_pair_sums(t1, t2)
        tau = top[PEER_TOPK - 1]
        z = jnp.exp(top[0] - top[0])
        for r in range(1, PEER_TOPK):
            z = z + jnp.exp(top[r] - top[0])
        z_scr[...] = z
        for r in range(PEER_TOPK):
            count_scr[r] = sum(jnp.where(c >= tau, 1.0, 0.0) for c in cand[r])

        def gates(h, carry):
            row = lambda ref, *idx: jnp.broadcast_to(ref[idx + (pl.ds(h, 1), slice(None))], (SUBLANE, tt))[None]
            s1s = f1_scr[h].reshape(nslab, SUBLANE, tt)
            s2s = nsel_scr[h].reshape(nslab, SUBLANE, tt)
            nsel = jnp.zeros((nslab, SUBLANE, tt), F32)
            rank = jnp.zeros((nslab, SUBLANE, tt), F32)
            for r in reversed(range(PEER_TOPK)):
                nsel = jnp.where(s1s >= row(top_scr, 0, r), row(count_scr, r), nsel)
            for r in range(PEER_TOPK):
                rank = jnp.where(row(top_scr, 1, r) > s2s, float(r + 1), rank)
            nsel_scr[h] = nsel.reshape(N_KEYS, tt)
            f1_scr[h] = jnp.exp(s1s - row(top_scr, 0, 0)).reshape(N_KEYS, tt)
            rank_b = rank.reshape(N_KEYS, tt).astype(BF16)
            f2_b = (jnp.exp(s2s - row(top_scr, 1, 0)) / row(z_scr)).reshape(N_KEYS, tt).astype(BF16)
            for j in range(nj):
                rank_scr[h, j] = rank_b[:, j * LANE:(j + 1) * LANE]
                f2_scr[h, j] = f2_b[:, j * LANE:(j + 1) * LANE]
            return carry

        lax.fori_loop(0, PEER_HEADS, gates, 0)

    def pre_activations(g):
        rows = slice(g * grp_rows, (g + 1) * grp_rows)
        pre = _nt(u_ref[rows, :], hn_scr[...])
        for j in range(nj):
            pre_scr[g % 2, j] = pre[:, j * LANE:(j + 1) * LANE]

    def gated_activations(g):
        for al in range(PEER_GROUP):
            a = e * (e_blk // N_KEYS) + g * PEER_GROUP + al
            rows = slice(al * N_KEYS, (al + 1) * N_KEYS)
            n_rows = [nsel_scr[h, pl.ds(a, 1), :] for h in range(PEER_HEADS)]
            f1_rows = [f1_scr[h, pl.ds(a, 1), :] for h in range(PEER_HEADS)]
            for j in range(nj):
                ln = slice(j * LANE, (j + 1) * LANE)
                gate = jnp.zeros((N_KEYS, LANE), BF16)
                for h in range(PEER_HEADS):
                    n_row = jnp.broadcast_to(n_rows[h][:, ln], (N_KEYS, LANE)).astype(BF16)
                    f1_row = jnp.broadcast_to(f1_rows[h][:, ln], (N_KEYS, LANE)).astype(BF16)
                    gate = gate + jnp.where(rank_scr[h, j] < n_row, f2_scr[h, j], 0.0) * f1_row
                p = pre_scr[g % 2, j, rows, :]
                gelu = 0.5 * p * (1.0 + lax.erf(p * (2.0 ** -0.5)))
                act_scr[g % 2, j, rows, :] = gelu.astype(BF16) * gate

    def accumulate(g):
        cols = slice(g * grp_rows, (g + 1) * grp_rows)
        act = jnp.concatenate([act_scr[g % 2, j] for j in range(nj)], axis=1)
        acc_scr[...] += _nn(vt_ref[:, cols], act)

    n_grp = e_blk // grp_rows
    pre_activations(0)
    for g in range(n_grp):
        if g + 1 < n_grp:
            pre_activations(g + 1)
        gated_activations(g)
        if g >= 1:
            accumulate(g - 1)
    accumulate(n_grp - 1)

    @pl.when(e == pl.num_programs(1) - 1)
    def _finish():
        y = x_ref[...] + acc_scr[...].T
        if final_norm:
            y = _rms(y, gfin_ref[...])
        y_ref[...] = y


def _peer(x, g, wpqt, keys, u, vtb, layer, gfin, final_norm, tt):
    t = x.shape[0]
    e_blk = PEER_E_BLK
    once = dict(pipeline_mode=pl.Buffered(1))
    return pl.pallas_call(
        functools.partial(_peer_kernel, final_norm=final_norm),
        grid=(t // tt, N_EXPERTS // e_blk),
        in_specs=[
            pl.BlockSpec((tt, D_MODEL), lambda i, e: (i, 0), **once),
            pl.BlockSpec((1, D_MODEL), lambda i, e: (0, 0)),
            pl.BlockSpec((2 * PEER_HEADS * N_KEYS, D_MODEL), lambda i, e: (0, 0), **once),
            pl.BlockSpec((2, N_KEYS, N_KEYS), lambda i, e: (0, 0, 0)),
            pl.BlockSpec((None, e_blk, D_MODEL), lambda i, e: (layer, e, 0)),
            pl.BlockSpec((None, None, D_MODEL, e_blk), lambda i, e: (layer, e, 0, 0)),
            pl.BlockSpec((1, D_MODEL), lambda i, e: (0, 0)),
        ],
        out_specs=pl.BlockSpec((tt, D_MODEL), lambda i, e: (i, 0), **once),
        out_shape=jax.ShapeDtypeStruct((t, D_MODEL), F32),
        scratch_shapes=[
            pltpu.VMEM((tt, D_MODEL), BF16),
            pltpu.VMEM((PEER_HEADS, tt // LANE, N_KEYS, LANE), BF16),
            pltpu.VMEM((PEER_HEADS, tt // LANE, N_KEYS, LANE), BF16),
            pltpu.VMEM((PEER_HEADS, N_KEYS, tt), F32),
            pltpu.VMEM((PEER_HEADS, N_KEYS, tt), F32),
            pltpu.VMEM((D_MODEL, tt), F32),
            pltpu.VMEM((2, tt // LANE, PEER_GROUP * N_KEYS, LANE), F32),
            pltpu.VMEM((2, tt // LANE, PEER_GROUP * N_KEYS, LANE), BF16),
            pltpu.VMEM((2, PEER_TOPK, PEER_HEADS, tt), F32),
            pltpu.VMEM((PEER_TOPK, PEER_HEADS, tt), F32),
            pltpu.VMEM((PEER_HEADS, tt), F32),
        ],
        compiler_params=_params("parallel", "arbitrary", vmem=VMEM_PHYSICAL * 15 // 16),
        name="peer",
    )(x, g, wpqt, keys, u, vtb, gfin)


def _split_w_in(w):
    sizes = (W_A, W_A, W_A, QK_L, QK_L, V_L, V_L, GATE_RANK, QK_L, QK_L, V_L, V_L)
    out, start = [], 0
    for n in sizes:
        out.append(w[:, start:start + n])
        start += n
    return out


def _layer_weights(w_in, w_gate2, b_gate, g_gla, w_out, w_pq):
    qa, ka, va, qb, kb, vb, rb, gb, qc, kc, vc, gc = _split_w_in(w_in)
    gb = jnp.pad(gb, ((0, 0), (0, LANE - GATE_RANK)))
    wa = jnp.concatenate([qa, ka, va], axis=1).astype(BF16)
    wl = jnp.concatenate([qb, qc, kb, kc, gb, vb, vc, rb, gc], axis=1).astype(BF16)
    wg = jnp.pad(w_gate2, ((0, LANE - GATE_RANK), (0, 0)))
    log_gamma = jnp.log(1.0 - 2.0 ** (-5.0 - jnp.arange(H_L, dtype=F32)))
    lgam = jnp.repeat(log_gamma, DK_L)
    return dict(
        wa=wa, wl=wl, wlt=wl.T, wg=wg, wgt=wg.T,
        wkvt=jnp.concatenate([ka, va], axis=1).T.astype(BF16),
        bg=b_gate.reshape(1, QK_L), bg_col=b_gate.reshape(QK_L, 1),
        lgam=lgam.reshape(1, QK_L), lgam_col=lgam.reshape(QK_L, 1),
        ggla=jnp.tile(g_gla, H_L).reshape(1, V_L), ggla_col=g_gla.reshape(DV_L, 1),
        w_out=w_out.astype(BF16), wpqt=w_pq.T.astype(BF16),
    )


def kernel(x_prompt, x_sample, cache_kv_win, state_gla, state_ret, w_in, w_gate2, b_gate, g_gla,
           w_out, g_mix, g_ffn, w_pq, sub_keys, u_tab, v_tab, g_final):
    batch, seq, _ = x_prompt.shape
    n_seq, t_new, _ = x_sample.shape
    depth = w_in.shape[0]
    win = min(MAX_WINDOW, seq)
    xp = x_prompt.reshape(batch * seq, D_MODEL)
    xs = x_sample.transpose(1, 0, 2).reshape(t_new * n_seq, D_MODEL)
    tables = _sample_attn_tables(t_new)
    gfin = g_final.reshape(1, D_MODEL)
    cache_t = cache_kv_win.transpose(0, 1, 3, 4, 5, 2)
    sg = state_gla.transpose(0, 2, 3, 4, 1).reshape(depth, H_L, DK_L * DV_L, n_seq)
    sr = state_ret.transpose(0, 2, 3, 4, 1).reshape(depth, H_L, DK_L * DV_L, n_seq)
    u_all = u_tab.astype(BF16)
    vtb_all = (v_tab.reshape(depth, N_EXPERTS // PEER_E_BLK, PEER_E_BLK, D_MODEL)
               .transpose(0, 1, 3, 2).astype(BF16))
    kv_p, kv_s, gla_p, gla_s, ret_p, ret_s = [], [], [], [], [], []
    for l in range(depth):
        w = _layer_weights(w_in[l], w_gate2[l], b_gate[l], g_gla[l], w_out[l], w_pq[l])
        gm = g_mix[l].reshape(1, D_MODEL)
        gf = g_ffn[l].reshape(1, D_MODEL)
        last = l == depth - 1

        za, zl = _in_proj(xp, gm, w["wa"], w["wl"])
        kv_p.append(_kv_tail(xp, gm, w["wkvt"], batch, seq, win).reshape(batch, 2, H_A, HEAD_DIM, win))
        oa = _attn_prompt(za, batch, seq)
        obc, sfin = _linattn_prompt(zl, w["wg"], w["bg"], w["lgam"], w["ggla"], batch, seq)
        xp = _out_proj_prompt(oa, obc, xp, w["w_out"])
        xp = _peer(xp, gf, w["wpqt"], sub_keys[l], u_all, vtb_all, l, gfin, last, tt=512)
        gla_p.append(sfin[:, 0])
        ret_p.append(sfin[:, 1])

        za_s, zlt = _in_proj_sample(xs, gm, w["wa"], w["wlt"])
        oa_s = _attn_sample(za_s, cache_t, l, tables, n_seq, t_new)
        obct, s1 = _linattn_sample(zlt, sg, sr, l, w["wgt"], w["bg_col"], w["lgam_col"], w["ggla_col"],
                                   n_seq, t_new)
        xs = _out_proj_sample(oa_s, obct, xs, w["w_out"])
        xs = _peer(xs, gf, w["wpqt"], sub_keys[l], u_all, vtb_all, l, gfin, last, tt=512)
        kv_new = za_s.reshape(t_new, n_seq, 3, H_A, HEAD_DIM)[:, :, 1:3].transpose(1, 0, 2, 3, 4)
        kv_s.append(kv_new)
        gla_s.append(s1[0])
        ret_s.append(s1[1])

    y_prompt = xp.reshape(batch, seq, D_MODEL)
    y_sample = xs.reshape(t_new, n_seq, D_MODEL).transpose(1, 0, 2)
    seq_major = lambda s: jnp.stack(s).reshape(depth, H_L, DK_L, DV_L, n_seq).transpose(0, 4, 1, 2, 3)
    kv_prompt = jnp.stack(kv_p).transpose(0, 1, 5, 2, 3, 4)
    return (y_prompt, y_sample, kv_prompt, jnp.stack(kv_s), jnp.stack(gla_p), seq_major(gla_s),
            jnp.stack(ret_p), seq_major(ret_s))
```
